```python
import math
import jax, jax.numpy as jnp
from jax import lax
import numpy as np

D_MODEL = 1024
BATCH = 8
SEQ = 8192
DEPTH = 1
DEC_BATCH = 128
DEC_SEQ = 8
PAST_LEN = 8192
PAGE_SIZE = 128

RW_HEADS = 8
RW_HEAD_DIM = 64
RW_WIDTH = RW_HEADS * RW_HEAD_DIM
DECAY_LORA = 64
AAA_LORA = 64
GATE_LORA = 128
RW_IN = 3 * RW_WIDTH + DECAY_LORA + AAA_LORA + GATE_LORA
GN_EPS = 64e-5
DA_HEADS = 4
DA_HEAD_DIM = 64
DA_V_DIM = 2 * DA_HEAD_DIM
DA_WIDTH = DA_HEADS * DA_V_DIM
ROPE_DIM = DA_HEAD_DIM // 4
ROPE_THETA = 500000.0
Q_BLOCK = 128
N_IN = RW_IN + 3 * DA_WIDTH + 2 * D_MODEL
N_EXPERTS = 32
TOP_K = 4
D_FF = D_MODEL
SWIGLU_ALPHA = 1.702
SWIGLU_LIMIT = 7.0
MOE_BLOCK = 128
NORM_EPS = 1e-5

kernel_name = "rwkv7_diffattn_gated_moe_step"

F32 = jnp.float32


def _lambda_init(l):
    return 0.8 - 0.6 * math.exp(-0.3 * l)


def _rmsnorm(x, g):
    xf = x.astype(F32)
    y = xf * lax.rsqrt(jnp.mean(xf * xf, axis=-1, keepdims=True) + NORM_EPS)
    return (y * g.astype(F32)).astype(x.dtype)


def _rope(x, pos):
    half = ROPE_DIM // 2
    inv = ROPE_THETA ** (-jnp.arange(half, dtype=F32) / half)
    ang = pos.astype(F32)[:, None] * inv[None, :]
    cos = jnp.cos(ang)[:, None, None, :]
    sin = jnp.sin(ang)[:, None, None, :]
    xr = x[..., :ROPE_DIM].astype(F32)
    x1, x2 = xr[..., :half], xr[..., half:]
    rot = jnp.concatenate([x1 * cos - x2 * sin, x2 * cos + x1 * sin], axis=-1)
    return jnp.concatenate([rot.astype(x.dtype), x[..., ROPE_DIM:]], axis=-1)


def _wkv_step(S, inp):
    r, w, k, v, a, b = inp
    sa = jnp.einsum("bhvk,bhk->bhv", S, a)
    S = S * w[:, :, None, :] + sa[..., None] * b[:, :, None, :] + v[..., None] * k[:, :, None, :]
    return S, jnp.einsum("bhvk,bhk->bhv", S, r)


def _rwkv_branch(zr, shift0, S0, lp):
    B, T, _ = zr.shape
    z_prev = jnp.concatenate([shift0[:, None, :].astype(zr.dtype), zr[:, :-1]], axis=1)
    zs = zr + (z_prev - zr) * lp["mu_shift"]
    r = zs[..., :RW_WIDTH]
    k = zs[..., RW_WIDTH:2 * RW_WIDTH]
    v = zs[..., 2 * RW_WIDTH:3 * RW_WIDTH]
    o = 3 * RW_WIDTH
    wd = zs[..., o:o + DECAY_LORA]
    ad = zs[..., o + DECAY_LORA:o + DECAY_LORA + AAA_LORA]
    gd = zs[..., o + DECAY_LORA + AAA_LORA:]
    w_log = -jax.nn.softplus(-(lp["w0"] + jnp.tanh(wd) @ lp["w_w2"]).astype(F32)) - 0.5
    decay = jnp.exp(-jnp.exp(w_log))
    a = jax.nn.sigmoid((lp["a0"] + ad @ lp["a_a2"]).astype(F32))
    g = jax.nn.sigmoid(gd) @ lp["g_g2"]
    heads = lambda t: t.reshape(B, T, RW_HEADS, RW_HEAD_DIM).astype(F32)
    kk = heads(k * lp["k_k"])
    kk = kk / jnp.maximum(jnp.sqrt(jnp.sum(kk * kk, axis=-1, keepdims=True)), 1e-12)
    k_mod = k.astype(F32) * (1.0 + (a - 1.0) * lp["k_a"])
    r_h, k_h, v_h, a_h, w_h = heads(r), heads(k_mod), heads(v), heads(a), heads(decay)
    tm = lambda t: jnp.moveaxis(t, 1, 0)
    S_T, ys = lax.scan(_wkv_step, S0.astype(F32),
                       (tm(r_h), tm(w_h), tm(k_h), tm(v_h), tm(-kk), tm(kk * a_h)))
    y = jnp.moveaxis(ys, 0, 1)
    mu = jnp.mean(y, axis=-1, keepdims=True)
    var = jnp.mean(jnp.square(y - mu), axis=-1, keepdims=True)
    y = ((y - mu) * lax.rsqrt(var + GN_EPS)).reshape(B, T, RW_WIDTH) * lp["ln_x_w"] + lp["ln_x_b"]
    bonus = jnp.sum(r_h * k_h * lp["r_k"], axis=-1, keepdims=True) * v_h
    y = (y + bonus.reshape(B, T, RW_WIDTH)) * g
    return y.astype(zr.dtype), S_T, zr[:, -1]


def _diff_weights(s, mask, lam):
    p = jax.nn.softmax(jnp.where(mask, s, -jnp.inf), axis=-1)
    return p[..., 0, :, :] - lam * p[..., 1, :, :]


def _attend_prompt(q, k, v, lam):
    B, T = q.shape[:2]
    nqb = T // Q_BLOCK
    qb = jnp.moveaxis(q.reshape(B, nqb, Q_BLOCK, DA_HEADS, 2, DA_HEAD_DIM), 1, 0)
    kpos = jnp.arange(T)
    scale = DA_HEAD_DIM ** -0.5

    def block(args):
        qi, start = args
        s = jnp.einsum("bqhcd,bkhcd->bhcqk", qi, k).astype(F32) * scale
        mask = kpos[None, :] <= (start + jnp.arange(Q_BLOCK))[:, None]
        attn = _diff_weights(s, mask, lam)
        return jnp.einsum("bhqk,bkhe->bqhe", attn.astype(v.dtype), v)

    o = lax.map(block, (qb, jnp.arange(nqb) * Q_BLOCK))
    return jnp.moveaxis(o, 0, 1).reshape(B, T, DA_HEADS, DA_V_DIM)


def _attend_sample(q, k, v, lam, cache_k, cache_v, page_table, layer):
    S = q.shape[1]
    past = page_table.shape[1] * PAGE_SIZE
    kpos = jnp.arange(past + S)
    qpos = past + jnp.arange(S)
    mask = kpos[None, :] <= qpos[:, None]
    scale = DA_HEAD_DIM ** -0.5

    def one(args):
        pt, qi, ki, vi = args
        kp = cache_k[layer, pt].reshape(past, DA_HEADS, 2, DA_HEAD_DIM)
        vp = cache_v[layer, pt].reshape(past, DA_HEADS, DA_V_DIM)
        ka = jnp.concatenate([kp, ki.astype(kp.dtype)], axis=0)
        va = jnp.concatenate([vp, vi.astype(vp.dtype)], axis=0)
        s = jnp.einsum("qhcd,khcd->hcqk", qi.astype(ka.dtype), ka).astype(F32) * scale
        attn = _diff_weights(s, mask, lam)
        return jnp.einsum("hqk,khe->qhe", attn.astype(va.dtype), va)

    return lax.map(one, (page_table, q, k, v))


def _moe(x2, w_router, b_router, w_up, b_up, w_down, b_down):
    N, D = x2.shape
    logits = (x2 @ w_router).astype(F32) + b_router
    top_v, top_e = lax.top_k(logits, TOP_K)
    gates = jax.nn.softmax(top_v, axis=-1)
    nk = N * TOP_K
    n_blocks = -(-(nk + N_EXPERTS * (MOE_BLOCK - 1)) // MOE_BLOCK)
    n_rows = n_blocks * MOE_BLOCK
    flat_e = top_e.reshape(-1)
    order = jnp.argsort(flat_e)
    sorted_e = flat_e[order]
    counts = jnp.bincount(flat_e, length=N_EXPERTS)
    padded = (counts + MOE_BLOCK - 1) // MOE_BLOCK * MOE_BLOCK
    start = jnp.cumsum(counts) - counts
    pend = jnp.cumsum(padded)
    pstart = pend - padded
    dest = pstart[sorted_e] + jnp.arange(nk) - start[sorted_e]
    row_tok = jnp.zeros((n_rows,), jnp.int32).at[dest].set((order // TOP_K).astype(jnp.int32))
    row_gate = jnp.zeros((n_rows,), F32).at[dest].set(gates.reshape(-1)[order])
    bstart = jnp.arange(n_blocks) * MOE_BLOCK
    block_e = jnp.minimum(jnp.sum(pend[None, :] <= bstart[:, None], axis=1), N_EXPERTS - 1)

    def expert_block(args):
        tok, gate, e = args
        h = x2[tok] @ w_up[e] + b_up[e]
        hg = jnp.minimum(h[:, :D_FF], SWIGLU_LIMIT)
        hl = jnp.clip(h[:, D_FF:], -SWIGLU_LIMIT, SWIGLU_LIMIT)
        act = hg * jax.nn.sigmoid(SWIGLU_ALPHA * hg) * (hl + 1.0)
        return (act @ w_down[e] + b_down[e]).astype(F32) * gate[:, None]

    ys = lax.map(expert_block, (row_tok.reshape(n_blocks, MOE_BLOCK),
                                row_gate.reshape(n_blocks, MOE_BLOCK), block_e))
    y = jnp.zeros((N, D), F32).at[row_tok].add(ys.reshape(n_rows, D))
    return y.astype(x2.dtype)


def _layer(x, pos, attend, S0, shift0, lp, lam_init):
    B, T, _ = x.shape
    xn = _rmsnorm(x, lp["norm1_g"])
    z = xn @ lp["w_in"]
    c0 = RW_IN
    c1 = c0 + DA_WIDTH
    c2 = c1 + DA_WIDTH
    c3 = c2 + DA_WIDTH
    c4 = c3 + D_MODEL
    zr, zq, zk, zv, zga, zgb = z[..., :c0], z[..., c0:c1], z[..., c1:c2], z[..., c2:c3], z[..., c3:c4], z[..., c4:]
    ya, S_T, shift_T = _rwkv_branch(zr, shift0, S0, lp)
    q = zq.reshape(B, T, DA_HEADS, 2, DA_HEAD_DIM)
    k = zk.reshape(B, T, DA_HEADS, 2, DA_HEAD_DIM)
    v = zv.reshape(B, T, DA_HEADS, DA_V_DIM)
    q = _rope(_rmsnorm(q, lp["q_norm_g"]), pos)
    k = _rope(_rmsnorm(k, lp["k_norm_g"]), pos)
    lam = (jnp.exp(jnp.sum(lp["lambda_q1"] * lp["lambda_k1"]).astype(F32))
           - jnp.exp(jnp.sum(lp["lambda_q2"] * lp["lambda_k2"]).astype(F32)) + lam_init)
    ob = attend(q, k, v, lam)
    ob = (_rmsnorm(ob, lp["subln_g"]) * (1.0 - lam_init)).reshape(B, T, DA_WIDTH)
    m = jax.nn.sigmoid(zga) * (ya @ lp["w_a"]) + jax.nn.sigmoid(zgb) * (ob.astype(x.dtype) @ lp["w_b"])
    h = x + (m @ lp["w_o"]).astype(x.dtype)
    hn = _rmsnorm(h, lp["norm2_g"]).reshape(B * T, D_MODEL)
    y = h + _moe(hn, lp["w_router"], lp["b_router"], lp["w_up"], lp["b_up"],
                 lp["w_down"], lp["b_down"]).reshape(B, T, D_MODEL)
    return y, k.reshape(B, T, DA_HEADS, 2 * DA_HEAD_DIM), v, S_T, shift_T


def setup_inputs(seed: int = 0) -> dict:
    key = jax.random.key(seed)
    ks = iter(jax.random.split(key, 40))
    nrm = lambda shape, scale: jax.random.normal(next(ks), shape, F32) * scale
    L = DEPTH
    n_pages = PAST_LEN // PAGE_SIZE
    n_used = DEC_BATCH * n_pages
    n_pool = n_used + (n_used + 3) // 4
    d = {}
    d["x_prompt"] = nrm((BATCH, SEQ, D_MODEL), 1.0)
    d["x_sample"] = nrm((DEC_BATCH, DEC_SEQ, D_MODEL), 1.0)
    d["cache_k"] = nrm((L, n_pool, PAGE_SIZE, DA_HEADS, 2 * DA_HEAD_DIM), 1.0)
    d["cache_v"] = nrm((L, n_pool, PAGE_SIZE, DA_HEADS, DA_V_DIM), 1.0)
    d["page_table"] = jax.random.permutation(next(ks), n_pool)[:n_used].reshape(DEC_BATCH, n_pages).astype(jnp.int32)
    d["state_wkv"] = nrm((L, DEC_BATCH, RW_HEADS, RW_HEAD_DIM, RW_HEAD_DIM), 0.5)
    d["state_shift"] = nrm((L, DEC_BATCH, RW_IN), 1.0)
    d["norm1_g"] = 1.0 + nrm((L, D_MODEL), 0.02)
    d["w_in"] = nrm((L, D_MODEL, N_IN), D_MODEL ** -0.5)
    d["mu_shift"] = jax.random.uniform(next(ks), (L, RW_IN), F32)
    d["w0"] = -3.0 + nrm((L, RW_WIDTH), 1.5)
    d["w_w2"] = nrm((L, DECAY_LORA, RW_WIDTH), 0.1 * DECAY_LORA ** -0.5)
    d["a0"] = nrm((L, RW_WIDTH), 0.5)
    d["a_a2"] = nrm((L, AAA_LORA, RW_WIDTH), 0.5 * AAA_LORA ** -0.5)
    d["g_g2"] = nrm((L, GATE_LORA, RW_WIDTH), GATE_LORA ** -0.5)
    d["k_k"] = 0.85 + nrm((L, RW_WIDTH), 0.05)
    d["k_a"] = 1.0 + nrm((L, RW_WIDTH), 0.05)
    d["r_k"] = nrm((L, RW_HEADS, RW_HEAD_DIM), 0.1)
    d["ln_x_w"] = 1.0 + nrm((L, RW_WIDTH), 0.02)
    d["ln_x_b"] = nrm((L, RW_WIDTH), 0.02)
    d["q_norm_g"] = 1.0 + nrm((L, DA_HEAD_DIM), 0.02)
    d["k_norm_g"] = 1.0 + nrm((L, DA_HEAD_DIM), 0.02)
    d["lambda_q1"] = nrm((L, DA_HEAD_DIM), 0.1)
    d["lambda_k1"] = nrm((L, DA_HEAD_DIM), 0.1)
    d["lambda_q2"] = nrm((L, DA_HEAD_DIM), 0.1)
    d["lambda_k2"] = nrm((L, DA_HEAD_DIM), 0.1)
    d["subln_g"] = 1.0 + nrm((L, DA_V_DIM), 0.02)
    d["w_a"] = nrm((L, RW_WIDTH, D_MODEL), RW_WIDTH ** -0.5)
    d["w_b"] = nrm((L, DA_WIDTH, D_MODEL), DA_WIDTH ** -0.5)
    d["w_o"] = nrm((L, D_MODEL, D_MODEL), D_MODEL ** -0.5)
    d["norm2_g"] = 1.0 + nrm((L, D_MODEL), 0.02)
    d["w_router"] = nrm((L, D_MODEL, N_EXPERTS), D_MODEL ** -0.5)
    d["b_router"] = nrm((L, N_EXPERTS), 0.01)
    d["w_up"] = nrm((L, N_EXPERTS, D_MODEL, 2 * D_FF), D_MODEL ** -0.5)
    d["b_up"] = nrm((L, N_EXPERTS, 2 * D_FF), 0.01)
    d["w_down"] = nrm((L, N_EXPERTS, D_FF, D_MODEL), D_FF ** -0.5)
    d["b_down"] = nrm((L, N_EXPERTS, D_MODEL), 0.01)
    return d


def reference(x_prompt, x_sample, cache_k, cache_v, page_table, state_wkv, state_shift,
              norm1_g, w_in, mu_shift, w0, w_w2, a0, a_a2, g_g2, k_k, k_a, r_k, ln_x_w, ln_x_b,
              q_norm_g, k_norm_g, lambda_q1, lambda_k1, lambda_q2, lambda_k2, subln_g,
              w_a, w_b, w_o, norm2_g, w_router, b_router, w_up, b_up, w_down, b_down):
    B, T, _ = x_prompt.shape
    S = x_sample.shape[1]
    past = page_table.shape[1] * PAGE_SIZE
    pos_prompt = jnp.arange(T)
    pos_sample = past + jnp.arange(S)
    yp, ys = x_prompt, x_sample
    kp_l, vp_l, wp_l, sp_l, ks_l, vs_l, ws_l, ss_l = [], [], [], [], [], [], [], []
    for l in range(DEPTH):
        lp = {"norm1_g": norm1_g[l], "w_in": w_in[l], "mu_shift": mu_shift[l], "w0": w0[l],
              "w_w2": w_w2[l], "a0": a0[l], "a_a2": a_a2[l], "g_g2": g_g2[l], "k_k": k_k[l],
              "k_a": k_a[l], "r_k": r_k[l], "ln_x_w": ln_x_w[l], "ln_x_b": ln_x_b[l],
              "q_norm_g": q_norm_g[l], "k_norm_g": k_norm_g[l], "lambda_q1": lambda_q1[l],
              "lambda_k1": lambda_k1[l], "lambda_q2": lambda_q2[l], "lambda_k2": lambda_k2[l],
              "subln_g": subln_g[l], "w_a": w_a[l], "w_b": w_b[l], "w_o": w_o[l],
              "norm2_g": norm2_g[l], "w_router": w_router[l], "b_router": b_router[l],
              "w_up": w_up[l], "b_up": b_up[l], "w_down": w_down[l], "b_down": b_down[l]}
        lam_init = _lambda_init(l)
        yp, kp, vp, wp, sp = _layer(
            yp, pos_prompt, _attend_prompt,
            jnp.zeros((B, RW_HEADS, RW_HEAD_DIM, RW_HEAD_DIM), F32),
            jnp.zeros((B, RW_IN), x_prompt.dtype), lp, lam_init)
        attend_s = lambda q, k, v, lam, l=l: _attend_sample(q, k, v, lam, cache_k, cache_v, page_table, l)
        ys, ks_, vs_, ws_, ss_ = _layer(ys, pos_sample, attend_s, state_wkv[l], state_shift[l], lp, lam_init)
        kp_l.append(kp); vp_l.append(vp); wp_l.append(wp); sp_l.append(sp)
        ks_l.append(ks_); vs_l.append(vs_); ws_l.append(ws_); ss_l.append(ss_)
    return (yp, ys, jnp.stack(kp_l), jnp.stack(vp_l), jnp.stack(wp_l), jnp.stack(sp_l),
            jnp.stack(ks_l), jnp.stack(vs_l), jnp.stack(ws_l), jnp.stack(ss_l))
```

```python
import functools
import math

import jax
import jax.numpy as jnp
import numpy as np
from jax import lax
from jax.experimental import pallas as pl
from jax.experimental.pallas import tpu as pltpu

F32 = jnp.float32
BF = jnp.bfloat16

D_MODEL = 1024
PAGE_SIZE = 128
RW_HEADS = 8
RW_HEAD_DIM = 64
RW_WIDTH = RW_HEADS * RW_HEAD_DIM
DECAY_LORA = 64
AAA_LORA = 64
GATE_LORA = 128
RW_IN = 3 * RW_WIDTH + DECAY_LORA + AAA_LORA + GATE_LORA
GN_EPS = 64e-5
DA_HEADS = 4
DA_HEAD_DIM = 64
DA_V_DIM = 2 * DA_HEAD_DIM
DA_WIDTH = DA_HEADS * DA_V_DIM
ROPE_DIM = DA_HEAD_DIM // 4
ROPE_THETA = 500000.0
N_IN = RW_IN + 3 * DA_WIDTH + 2 * D_MODEL
N_EXPERTS = 32
TOP_K = 4
D_FF = D_MODEL
SWIGLU_ALPHA = 1.702
SWIGLU_LIMIT = 7.0
NORM_EPS = 1e-5
LAMBDA_INIT = 0.8 - 0.6 * math.exp(-0.3 * 0)

VMEM_LIMIT = 56 * 1024 * 1024
MOE_ROWS = 256
GATHER_CHUNK = 512

_NT = (((1,), (1,)), ((), ()))
_TN = (((0,), (0,)), ((), ()))


def _cp(sem, vmem=VMEM_LIMIT):
    return pltpu.CompilerParams(dimension_semantics=sem, vmem_limit_bytes=vmem)


def _dot(a, b):
    return jnp.dot(a, b, preferred_element_type=F32)


def _dg(a, b, dims):
    return lax.dot_general(a, b, dims, preferred_element_type=F32)


def _split(x):
    hi = x.astype(BF)
    lo = (x - hi.astype(F32)).astype(BF)
    return hi, lo


def _dot3(a, b, dims=(((1,), (0,)), ((), ()))):
    ah, al = _split(a)
    bh, bl = _split(b)
    return _dg(ah, bh, dims) + _dg(ah, bl, dims) + _dg(al, bh, dims)


def _dot_exact_rhs(a, b_bf16):
    ah, al = _split(a)
    return _dot(ah, b_bf16) + _dot(al, b_bf16)


def _sigmoid(x):
    return 1.0 / (1.0 + jnp.exp(-x))


_SEGS = (0, RW_IN, RW_IN + DA_WIDTH, RW_IN + 2 * DA_WIDTH, RW_IN + 3 * DA_WIDTH, N_IN)


def _inproj_kernel(x_ref, g_ref, w_ref, zr_ref, zq_ref, zk_ref, zv_ref, zvb_ref, zg_ref):
    x = x_ref[...]
    xn = x * lax.rsqrt(jnp.mean(x * x, axis=-1, keepdims=True) + NORM_EPS) * g_ref[...]
    xb = xn.astype(BF)
    zr_ref[...] = _dot(xb, w_ref[:, _SEGS[0]:_SEGS[1]])
    zq_ref[...] = _dot(xb, w_ref[:, _SEGS[1]:_SEGS[2]])
    zk_ref[...] = _dot(xb, w_ref[:, _SEGS[2]:_SEGS[3]])
    zv = _dot(xb, w_ref[:, _SEGS[3]:_SEGS[4]])
    zv_ref[...] = zv
    zvb_ref[...] = zv.astype(BF)
    zg_ref[...] = _dot(xb, w_ref[:, _SEGS[4]:_SEGS[5]])


def _inproj(x2, g, w_bf, tm):
    n = x2.shape[0]
    row = lambda w: pl.BlockSpec((tm, w), lambda i: (i, 0))
    return pl.pallas_call(
        _inproj_kernel,
        grid=(n // tm,),
        in_specs=[row(D_MODEL),
                  pl.BlockSpec((1, D_MODEL), lambda i: (0, 0)),
                  pl.BlockSpec((D_MODEL, N_IN), lambda i: (0, 0))],
        out_specs=[row(RW_IN), row(DA_WIDTH), row(DA_WIDTH), row(DA_WIDTH), row(DA_WIDTH),
                   row(2 * D_MODEL)],
        out_shape=[jax.ShapeDtypeStruct((n, RW_IN), F32),
                   jax.ShapeDtypeStruct((n, DA_WIDTH), F32),
                   jax.ShapeDtypeStruct((n, DA_WIDTH), F32),
                   jax.ShapeDtypeStruct((n, DA_WIDTH), F32),
                   jax.ShapeDtypeStruct((n, DA_WIDTH), BF),
                   jax.ShapeDtypeStruct((n, 2 * D_MODEL), F32)],
        compiler_params=_cp(("parallel",)),
        name="inproj",
    )(x2, g, w_bf)


def _group_ones(width, group):
    i = np.arange(width)
    return jnp.asarray((i[:, None] // group) == (i[None, :] // group), dtype=BF)


def _qk_prep_kernel(zq_ref, zk_ref, gq_ref, gk_ref, cos_ref, sa_ref, sb_ref, j_ref,
                    q_ref, k_ref, kb_ref):
    cos = jnp.concatenate([cos_ref[...]] * (DA_WIDTH // 128), axis=-1)
    sa = jnp.concatenate([sa_ref[...]] * (DA_WIDTH // 128), axis=-1)
    sb = jnp.concatenate([sb_ref[...]] * (DA_WIDTH // 128), axis=-1)

    def norm_rope(z, g):
        ms = _dot_exact_rhs(z * z, j_ref[...]) * (1.0 / DA_HEAD_DIM)
        y = z * lax.rsqrt(ms + NORM_EPS) * g
        half = ROPE_DIM // 2
        up = pltpu.roll(y, DA_WIDTH - half, axis=1)
        dn = pltpu.roll(y, half, axis=1)
        return y * cos + up * sa + dn * sb

    q = norm_rope(zq_ref[0], gq_ref[...])
    k = norm_rope(zk_ref[0], gk_ref[...])
    q_ref[0] = (q * (DA_HEAD_DIM ** -0.5)).astype(BF)
    k_ref[0] = k
    kb_ref[0] = k.astype(BF)


def _rope_tables(pos):
    half = ROPE_DIM // 2
    inv = ROPE_THETA ** (-jnp.arange(half, dtype=F32) / half)
    ang = pos.astype(F32)[:, None] * inv[None, :]
    lane = np.arange(128) % DA_HEAD_DIM
    idx = lane % half
    c = jnp.cos(ang)[:, idx]
    s = jnp.sin(ang)[:, idx]
    in_rope = jnp.asarray(lane < ROPE_DIM)
    first = jnp.asarray(lane < half)
    second = jnp.asarray((lane >= half) & (lane < ROPE_DIM))
    cos_t = jnp.where(in_rope[None, :], c, 1.0)
    sa_t = jnp.where(first[None, :], -s, 0.0)
    sb_t = jnp.where(second[None, :], s, 0.0)
    return cos_t, sa_t, sb_t


def _qk_prep(zq, zk, gq, gk, pos, tt):
    b, t, _ = zq.shape
    cos_t, sa_t, sb_t = _rope_tables(pos)
    gq_t = jnp.tile(gq, DA_WIDTH // DA_HEAD_DIM)[None, :]
    gk_t = jnp.tile(gk, DA_WIDTH // DA_HEAD_DIM)[None, :]
    tok = pl.BlockSpec((1, tt, DA_WIDTH), lambda i, j: (i, j, 0))
    tab = pl.BlockSpec((tt, 128), lambda i, j: (j, 0))
    par = pl.BlockSpec((1, DA_WIDTH), lambda i, j: (0, 0))
    return pl.pallas_call(
        _qk_prep_kernel,
        grid=(b, t // tt),
        in_specs=[tok, tok, par, par, tab, tab, tab,
                  pl.BlockSpec((DA_WIDTH, DA_WIDTH), lambda i, j: (0, 0))],
        out_specs=[tok, tok, tok],
        out_shape=[jax.ShapeDtypeStruct((b, t, DA_WIDTH), BF),
                   jax.ShapeDtypeStruct((b, t, DA_WIDTH), F32),
                   jax.ShapeDtypeStruct((b, t, DA_WIDTH), BF)],
        compiler_params=_cp(("parallel", "parallel")),
        name="qk_prep",
    )(zq, zk, gq_t, gk_t, cos_t, sa_t, sb_t, _group_ones(DA_WIDTH, DA_HEAD_DIM))


def _rwkv_prep_kernel(zr_ref, sh_ref, mu_ref, w0_ref, a0_ref, kk_ref, ka_ref,
                      w2_ref, a2_ref, g2_ref, j_ref,
                      r_ref, lw_ref, k_ref, v_ref, a_ref, b_ref, g_ref, carry_ref):
    tt = zr_ref.shape[1]

    @pl.when(pl.program_id(1) == 0)
    def _():
        carry_ref[...] = sh_ref[0]

    z = zr_ref[0]
    rolled = pltpu.roll(z, 1, axis=0)
    first_row = lax.broadcasted_iota(jnp.int32, (tt, 1), 0) == 0
    zprev = jnp.where(first_row, carry_ref[...], rolled)
    carry_ref[...] = z[tt - 1:tt, :]
    zs = z + (zprev - z) * mu_ref[...]

    r = zs[:, 0:RW_WIDTH]
    k = zs[:, RW_WIDTH:2 * RW_WIDTH]
    v = zs[:, 2 * RW_WIDTH:3 * RW_WIDTH]
    wa = zs[:, 3 * RW_WIDTH:3 * RW_WIDTH + DECAY_LORA + AAA_LORA]
    gd = zs[:, 3 * RW_WIDTH + DECAY_LORA + AAA_LORA:]

    lw = w0_ref[...] + _dot(jnp.tanh(wa).astype(BF), w2_ref[...])
    log_decay = -math.exp(-0.5) * _sigmoid(lw)
    a = _sigmoid(a0_ref[...] + _dot(wa.astype(BF), a2_ref[...]))
    g = _dot(_sigmoid(gd).astype(BF), g2_ref[...])
    kk = k * kk_ref[...]
    ss = _dot_exact_rhs(kk * kk, j_ref[...])
    kk = kk / jnp.maximum(jnp.sqrt(ss), 1e-12)
    kmod = k * (1.0 + (a - 1.0) * ka_ref[...])

    r_ref[0] = r
    lw_ref[0] = log_decay
    k_ref[0] = kmod
    v_ref[0] = v
    a_ref[0] = -kk
    b_ref[0] = kk * a
    g_ref[0] = g


def _rwkv_prep(zr, shift0, p, tt):
    b, t, _ = zr.shape
    zero_pad = jnp.zeros((DECAY_LORA, RW_WIDTH), F32)
    w2p = jnp.concatenate([p["w_w2"], zero_pad], axis=0).astype(BF)
    a2p = jnp.concatenate([zero_pad, p["a_a2"]], axis=0).astype(BF)
    tok = lambda w: pl.BlockSpec((1, tt, w), lambda i, j: (i, j, 0))
    par = lambda w: pl.BlockSpec((1, w), lambda i, j: (0, 0))
    mat = lambda r, c: pl.BlockSpec((r, c), lambda i, j: (0, 0))
    return pl.pallas_call(
        _rwkv_prep_kernel,
        grid=(b, t // tt),
        in_specs=[tok(RW_IN), pl.BlockSpec((1, 1, RW_IN), lambda i, j: (i, 0, 0)),
                  par(RW_IN), par(RW_WIDTH), par(RW_WIDTH), par(RW_WIDTH), par(RW_WIDTH),
                  mat(DECAY_LORA + AAA_LORA, RW_WIDTH), mat(DECAY_LORA + AAA_LORA, RW_WIDTH),
                  mat(GATE_LORA, RW_WIDTH), mat(RW_WIDTH, RW_WIDTH)],
        out_specs=[tok(RW_WIDTH)] * 7,
        out_shape=[jax.ShapeDtypeStruct((b, t, RW_WIDTH), F32)] * 7,
        scratch_shapes=[pltpu.VMEM((1, RW_IN), F32)],
        compiler_params=_cp(("parallel", "arbitrary")),
        name="rwkv_prep",
    )(zr, shift0[:, None, :], p["mu_shift"][None, :], p["w0"][None, :], p["a0"][None, :],
      p["k_k"][None, :], p["k_a"][None, :], w2p, a2p, p["g_g2"].astype(BF),
      _group_ones(RW_WIDTH, RW_HEAD_DIM))


def _rwkv_chunk_kernel(r_ref, lw_ref, k_ref, v_ref, a_ref, b_ref, g_ref, s0_ref,
                       lnw_ref, lnb_ref, rk_ref, y_ref, sT_ref, s_ref):
    c = r_ref.shape[1]
    n = RW_HEAD_DIM

    @pl.when(pl.program_id(1) == 0)
    def _():
        s_ref[...] = s0_ref[0]

    r = r_ref[0]
    lw = lw_ref[0]
    k = k_ref[0]
    v = v_ref[0]
    a = a_ref[0]
    b = b_ref[0]

    row = lax.broadcasted_iota(jnp.int32, (c, c), 0)
    col = lax.broadcasted_iota(jnp.int32, (c, c), 1)
    lower = row >= col
    strict = row > col
    eye_c = (row == col).astype(F32)
    tri = lower.astype(F32)
    cum = _dot3(tri, lw)
    cum_last = cum[c - 1:c, :]
    rt = r * jnp.exp(cum)
    at = a * jnp.exp(cum - lw)
    pinv = jnp.exp(-cum)
    bt = b * pinv
    kt = k * pinv
    pc = jnp.exp(cum_last - cum)
    bh = b * pc
    kh = k * pc
    p_chunk = jnp.exp(cum_last)
    bonus_rk = r * k * rk_ref[...]

    krow = lax.broadcasted_iota(jnp.int32, (n, n), 0)
    kcol = lax.broadcasted_iota(jnp.int32, (n, n), 1)
    eye_n = krow == kcol

    outs = []
    for h in range(RW_HEADS):
        sl = slice(h * n, (h + 1) * n)
        at_h, rt_h, bt_h, kt_h = at[:, sl], rt[:, sl], bt[:, sl], kt[:, sl]
        bh_h, kh_h, v_h = bh[:, sl], kh[:, sl], v[:, sl]
        a_ab = jnp.where(strict, _dot3(at_h, bt_h, _NT), 0.0)
        a_ak = jnp.where(strict, _dot3(at_h, kt_h, _NT), 0.0)
        a_rb = jnp.where(lower, _dot3(rt_h, bt_h, _NT), 0.0)
        a_rk = jnp.where(lower, _dot3(rt_h, kt_h, _NT), 0.0)
        t_inv = eye_c + a_ab
        pw = a_ab
        for _ in range(int(math.log2(c)) - 1):
            pw = _dot3(pw, pw)
            t_inv = t_inv + _dot3(t_inv, pw)
        w1 = _dot3(t_inv, at_h)
        u0 = _dot3(t_inv, _dot3(a_ak, v_h))
        m_mat = jnp.where(eye_n, p_chunk[:, sl], 0.0) + _dot3(w1, bh_h, _TN)
        n_mat = _dot3(u0, bh_h, _TN) + _dot3(v_h, kh_h, _TN)
        g_mat = rt_h + _dot3(a_rb, w1)
        y0 = _dot3(a_rb, u0) + _dot3(a_rk, v_h)
        s_h = s_ref[h]
        y = _dot3(g_mat, s_h, _NT) + y0
        s_ref[h] = _dot3(s_h, m_mat) + n_mat

        mu = jnp.mean(y, axis=-1, keepdims=True)
        var = jnp.mean(jnp.square(y - mu), axis=-1, keepdims=True)
        yn = (y - mu) * lax.rsqrt(var + GN_EPS)
        bonus = jnp.sum(bonus_rk[:, sl], axis=-1, keepdims=True) * v_h
        outs.append((yn, bonus))

    yn = jnp.concatenate([o[0] for o in outs], axis=-1)
    bonus = jnp.concatenate([o[1] for o in outs], axis=-1)
    y_ref[0] = ((yn * lnw_ref[...] + lnb_ref[...] + bonus) * g_ref[0]).astype(BF)

    @pl.when(pl.program_id(1) == pl.num_programs(1) - 1)
    def _():
        sT_ref[0] = s_ref[...]


def _rwkv_chunk(r, lw, k, v, a, b, g, s0, p, c):
    bsz, t, _ = r.shape
    tok = pl.BlockSpec((1, c, RW_WIDTH), lambda i, j: (i, j, 0))
    par = pl.BlockSpec((1, RW_WIDTH), lambda i, j: (0, 0))
    st = pl.BlockSpec((1, RW_HEADS, RW_HEAD_DIM, RW_HEAD_DIM), lambda i, j: (i, 0, 0, 0))
    return pl.pallas_call(
        _rwkv_chunk_kernel,
        grid=(bsz, t // c),
        in_specs=[tok] * 7 + [st, par, par, par],
        out_specs=[tok, st],
        out_shape=[jax.ShapeDtypeStruct((bsz, t, RW_WIDTH), BF),
                   jax.ShapeDtypeStruct((bsz, RW_HEADS, RW_HEAD_DIM, RW_HEAD_DIM), F32)],
        scratch_shapes=[pltpu.VMEM((RW_HEADS, RW_HEAD_DIM, RW_HEAD_DIM), F32)],
        compiler_params=_cp(("parallel", "arbitrary")),
        name="rwkv_chunk",
    )(r, lw, k, v, a, b, g, s0, p["ln_x_w"][None, :], p["ln_x_b"][None, :],
      p["r_k"].reshape(1, RW_WIDTH))


def _lambda_full(lq1, lk1, lq2, lk2):
    s1 = jnp.sum(lq1 * lk1, axis=-1, keepdims=True)
    s2 = jnp.sum(lq2 * lk2, axis=-1, keepdims=True)
    return jnp.exp(s1) - jnp.exp(s2) + LAMBDA_INIT


def _subln(o, g):
    y = o * lax.rsqrt(jnp.mean(o * o, axis=-1, keepdims=True) + NORM_EPS)
    return y * g * (1.0 - LAMBDA_INIT)


def _prompt_attn_kernel(qi_ref, ki_ref, q_ref, k_ref, v_ref, lq1, lk1, lq2, lk2, sg_ref,
                        o_ref, qm_ref, m_ref, l_ref, acc_ref):
    p = pl.program_id(2)
    qi = qi_ref[p]
    ki = ki_ref[p]
    tq = q_ref.shape[1]
    tk = k_ref.shape[1]

    @pl.when(ki == 0)
    def _():
        q = q_ref[0]
        lane = lax.broadcasted_iota(jnp.int32, q.shape, 1)
        zero = jnp.zeros_like(q)
        qm_ref[0] = jnp.where(lane < DA_HEAD_DIM, q, zero)
        qm_ref[1] = jnp.where(lane >= DA_HEAD_DIM, q, zero)
        m_ref[...] = jnp.full(m_ref.shape, -jnp.inf, F32)
        l_ref[...] = jnp.zeros(l_ref.shape, F32)
        acc_ref[...] = jnp.zeros(acc_ref.shape, F32)

    def step(masked):
        kb = k_ref[0]
        vb = v_ref[0]
        if masked:
            rowi = lax.broadcasted_iota(jnp.int32, (tq, tk), 0)
            coli = lax.broadcasted_iota(jnp.int32, (tq, tk), 1)
            keep = coli <= rowi
        for c in range(2):
            s = _dg(qm_ref[c], kb, _NT)
            if masked:
                s = jnp.where(keep, s, -jnp.inf)
            m_old = m_ref[c]
            m_new = jnp.maximum(m_old, jnp.max(s, axis=-1, keepdims=True))
            alpha = jnp.exp(m_old - m_new)
            pr = jnp.exp(s - m_new)
            l_ref[c] = alpha * l_ref[c] + jnp.sum(pr, axis=-1, keepdims=True)
            acc_ref[c] = alpha * acc_ref[c] + _dot(pr.astype(BF), vb)
            m_ref[c] = m_new

    @pl.when(ki < qi)
    def _():
        step(False)

    @pl.when(ki == qi)
    def _():
        step(True)
        lam = _lambda_full(lq1[...], lk1[...], lq2[...], lk2[...])
        o = acc_ref[0] / l_ref[0] - lam * (acc_ref[1] / l_ref[1])
        o_ref[0] = _subln(o, sg_ref[...]).astype(BF)


def _prompt_attention(q, kb, vb, p, tq):
    bsz, t, _ = q.shape
    nq = t // tq
    qi = np.concatenate([np.full(i + 1, i) for i in range(nq)]).astype(np.int32)
    ki = np.concatenate([np.arange(i + 1) for i in range(nq)]).astype(np.int32)
    lam_spec = pl.BlockSpec((1, DA_HEAD_DIM), lambda b, h, s, qt, kt: (0, 0))
    grid_spec = pltpu.PrefetchScalarGridSpec(
        num_scalar_prefetch=2,
        grid=(bsz, DA_HEADS, len(qi)),
        in_specs=[pl.BlockSpec((1, tq, DA_V_DIM), lambda b, h, s, qt, kt: (b, qt[s], h)),
                  pl.BlockSpec((1, tq, DA_V_DIM), lambda b, h, s, qt, kt: (b, kt[s], h)),
                  pl.BlockSpec((1, tq, DA_V_DIM), lambda b, h, s, qt, kt: (b, kt[s], h)),
                  lam_spec, lam_spec, lam_spec, lam_spec,
                  pl.BlockSpec((1, DA_V_DIM), lambda b, h, s, qt, kt: (0, 0))],
        out_specs=pl.BlockSpec((1, tq, DA_V_DIM), lambda b, h, s, qt, kt: (b, qt[s], h)),
        scratch_shapes=[pltpu.VMEM((2, tq, DA_V_DIM), BF),
                        pltpu.VMEM((2, tq, 1), F32),
                        pltpu.VMEM((2, tq, 1), F32),
                        pltpu.VMEM((2, tq, DA_V_DIM), F32)])
    return pl.pallas_call(
        _prompt_attn_kernel,
        grid_spec=grid_spec,
        out_shape=jax.ShapeDtypeStruct((bsz, t, DA_WIDTH), BF),
        compiler_params=_cp(("parallel", "parallel", "arbitrary")),
        name="prompt_attn",
    )(jnp.asarray(qi), jnp.asarray(ki), q, kb, vb,
      p["lambda_q1"][None, :], p["lambda_k1"][None, :], p["lambda_q2"][None, :],
      p["lambda_k2"][None, :], p["subln_g"][None, :])


def _make_sample_attn_kernel(pages):
    n_rows = DA_HEADS * 2

    def kernel(pt_ref, q_ref, kn_ref, vn_ref, lq1, lk1, lq2, lk2, sg_ref, *rest):
        k_refs = rest[:pages]
        v_refs = rest[pages:2 * pages]
        o_ref = rest[2 * pages]
        qbd_ref, m_ref, l_ref, acc_ref = rest[2 * pages + 1:]
        j = pl.program_id(1)
        s_len = q_ref.shape[1]
        rows = n_rows * s_len

        @pl.when(j == 0)
        def _():
            q = q_ref[0]
            qt = jnp.concatenate([q] * n_rows, axis=0)
            r = lax.broadcasted_iota(jnp.int32, (rows, DA_WIDTH), 0)
            ln = lax.broadcasted_iota(jnp.int32, (rows, DA_WIDTH), 1)
            qbd_ref[...] = jnp.where(ln // DA_HEAD_DIM == r // s_len, qt, jnp.zeros_like(qt))
            m_ref[...] = jnp.full(m_ref.shape, -jnp.inf, F32)
            l_ref[...] = jnp.zeros(l_ref.shape, F32)
            acc_ref[...] = jnp.zeros(acc_ref.shape, F32)

        def update(s, vals):
            m_old = m_ref[...]
            m_new = jnp.maximum(m_old, jnp.max(s, axis=-1, keepdims=True))
            alpha = jnp.exp(m_old - m_new)
            pr = jnp.exp(s - m_new)
            l_ref[...] = alpha * l_ref[...] + jnp.sum(pr, axis=-1, keepdims=True)
            acc_ref[...] = alpha * acc_ref[...] + _dot(pr.astype(BF), vals)
            m_ref[...] = m_new

        qbd = qbd_ref[...]
        s_all = jnp.concatenate([_dg(qbd, kr[0].astype(BF), _NT) for kr in k_refs], axis=-1)
        v_all = jnp.concatenate([vr[0].astype(BF) for vr in v_refs], axis=0)
        update(s_all, v_all)

        @pl.when(j == pl.num_programs(1) - 1)
        def _():
            s = _dg(qbd, kn_ref[0], _NT)
            qpos = lax.broadcasted_iota(jnp.int32, (rows, s_len), 0) % s_len
            kpos = lax.broadcasted_iota(jnp.int32, (rows, s_len), 1)
            update(jnp.where(kpos <= qpos, s, -jnp.inf), vn_ref[0])
            lam = _lambda_full(lq1[...], lk1[...], lq2[...], lk2[...])
            o = acc_ref[...] / l_ref[...]
            heads = []
            for h in range(DA_HEADS):
                o1 = o[(2 * h) * s_len:(2 * h + 1) * s_len, h * DA_V_DIM:(h + 1) * DA_V_DIM]
                o2 = o[(2 * h + 1) * s_len:(2 * h + 2) * s_len, h * DA_V_DIM:(h + 1) * DA_V_DIM]
                heads.append(_subln(o1 - lam * o2, sg_ref[...]))
            o_ref[0] = jnp.concatenate(heads, axis=-1).astype(BF)

    return kernel


def _sample_attention(q, kb, vb, cache_k, cache_v, page_table, p, pages):
    bsz, s_len, _ = q.shape
    n_pages = page_table.shape[1]
    ck = cache_k.reshape(cache_k.shape[0], PAGE_SIZE, DA_WIDTH)
    cv = cache_v.reshape(cache_v.shape[0], PAGE_SIZE, DA_WIDTH)
    rows = DA_HEADS * 2 * s_len
    tok = pl.BlockSpec((1, s_len, DA_WIDTH), lambda b, j, pt: (b, 0, 0))
    lam_spec = pl.BlockSpec((1, DA_HEAD_DIM), lambda b, j, pt: (0, 0))

    def page_spec(i):
        return pl.BlockSpec((1, PAGE_SIZE, DA_WIDTH),
                            lambda b, j, pt: (pt[b * n_pages + j * pages + i], 0, 0))

    grid_spec = pltpu.PrefetchScalarGridSpec(
        num_scalar_prefetch=1,
        grid=(bsz, n_pages // pages),
        in_specs=[tok, tok, tok, lam_spec, lam_spec, lam_spec, lam_spec,
                  pl.BlockSpec((1, DA_V_DIM), lambda b, j, pt: (0, 0))]
                 + [page_spec(i) for i in range(pages)] * 2,
        out_specs=tok,
        scratch_shapes=[pltpu.VMEM((rows, DA_WIDTH), BF),
                        pltpu.VMEM((rows, 1), F32),
                        pltpu.VMEM((rows, 1), F32),
                        pltpu.VMEM((rows, DA_WIDTH), F32)])
    return pl.pallas_call(
        _make_sample_attn_kernel(pages),
        grid_spec=grid_spec,
        out_shape=jax.ShapeDtypeStruct((bsz, s_len, DA_WIDTH), BF),
        compiler_params=_cp(("parallel", "arbitrary")),
        name="sample_attn",
    )(page_table.reshape(-1), q, kb, vb,
      p["lambda_q1"][None, :], p["lambda_k1"][None, :], p["lambda_q2"][None, :],
      p["lambda_k2"][None, :], p["subln_g"][None, :], *([ck] * pages), *([cv] * pages))


def _merge_kernel(ya_ref, ob_ref, zg_ref, x_ref, wa_ref, wb_ref, wo_ref, g2_ref,
                  wr_hi_ref, wr_lo_ref, br_ref, h_ref, hn_ref, te_ref, tg_ref):
    zg = zg_ref[...]
    m = (_sigmoid(zg[:, :D_MODEL]) * _dot(ya_ref[...], wa_ref[...])
         + _sigmoid(zg[:, D_MODEL:]) * _dot(ob_ref[...], wb_ref[...]))
    h = x_ref[...] + _dot(m.astype(BF), wo_ref[...])
    h_ref[...] = h
    hn = h * lax.rsqrt(jnp.mean(h * h, axis=-1, keepdims=True) + NORM_EPS) * g2_ref[...]
    hn_ref[...] = hn

    hh, hl = _split(hn)
    logits = (_dg(wr_hi_ref[...], hh, _NT) + _dg(wr_hi_ref[...], hl, _NT)
              + _dg(wr_lo_ref[...], hh, _NT)) + br_ref[...]
    eidx = lax.broadcasted_iota(jnp.int32, logits.shape, 0)
    vals, idxs = [], []
    for _ in range(TOP_K):
        mx = jnp.max(logits, axis=0, keepdims=True)
        am = jnp.min(jnp.where(logits == mx, eidx, N_EXPERTS), axis=0, keepdims=True)
        vals.append(mx)
        idxs.append(am)
        logits = jnp.where(eidx == am, -jnp.inf, logits)
    ex = [jnp.exp(vv - vals[0]) for vv in vals]
    den = ex[0] + ex[1] + ex[2] + ex[3]
    te_ref[...] = jnp.concatenate(idxs, axis=0)
    tg_ref[...] = jnp.concatenate([e / den for e in ex], axis=0)


def _merge(ya, ob, zg, x2, p, tm):
    n = x2.shape[0]
    wr_t = p["w_router"].T
    wr_hi = wr_t.astype(BF)
    wr_lo = (wr_t - wr_hi.astype(F32)).astype(BF)
    row = lambda w: pl.BlockSpec((tm, w), lambda i: (i, 0))
    mat = lambda r, c: pl.BlockSpec((r, c), lambda i: (0, 0))
    colblk = pl.BlockSpec((TOP_K, tm), lambda i: (0, i))
    return pl.pallas_call(
        _merge_kernel,
        grid=(n // tm,),
        in_specs=[row(RW_WIDTH), row(DA_WIDTH), row(2 * D_MODEL), row(D_MODEL),
                  mat(RW_WIDTH, D_MODEL), mat(DA_WIDTH, D_MODEL), mat(D_MODEL, D_MODEL),
                  mat(1, D_MODEL), mat(N_EXPERTS, D_MODEL), mat(N_EXPERTS, D_MODEL),
                  mat(N_EXPERTS, 1)],
        out_specs=[row(D_MODEL), row(D_MODEL), colblk, colblk],
        out_shape=[jax.ShapeDtypeStruct((n, D_MODEL), F32),
                   jax.ShapeDtypeStruct((n, D_MODEL), F32),
                   jax.ShapeDtypeStruct((TOP_K, n), jnp.int32),
                   jax.ShapeDtypeStruct((TOP_K, n), F32)],
        compiler_params=_cp(("parallel",)),
        name="merge_router",
    )(ya, ob, zg, x2, p["w_a"].astype(BF), p["w_b"].astype(BF), p["w_o"].astype(BF),
      p["norm2_g"][None, :], wr_hi, wr_lo, p["b_router"][:, None])


def _row_copy(src_ref, out_ref, sem, src_row, dst_row):
    return pltpu.make_async_copy(src_ref.at[pl.ds(src_row, 1)], out_ref.at[pl.ds(dst_row, 1)], sem)


def _gather_kernel(idx_ref, src_ref, out_ref, idx_smem, idx_sem, row_sem):
    i = pl.program_id(0)
    n = pl.num_programs(0)
    ch = idx_smem.shape[1]
    slot = i % 2

    def idx_copy(chunk, s):
        return pltpu.make_async_copy(idx_ref.at[chunk], idx_smem.at[s], idx_sem.at[s])

    @pl.when(i == 0)
    def _():
        idx_copy(0, 0).start()

    idx_copy(i, slot).wait()

    @pl.when(i + 1 < n)
    def _():
        idx_copy(i + 1, 1 - slot).start()

    def issue(r, carry):
        _row_copy(src_ref, out_ref, row_sem.at[slot], idx_smem[slot, r], i * ch + r).start()
        return carry

    lax.fori_loop(0, ch, issue, 0)

    def drain(s):
        def body(r, carry):
            _row_copy(src_ref, out_ref, row_sem.at[s], 0, 0).wait()
            return carry
        lax.fori_loop(0, ch, body, 0)

    @pl.when(i > 0)
    def _():
        drain(1 - slot)

    @pl.when(i == n - 1)
    def _():
        drain(slot)


def _row_gather(src, idx):
    m = idx.shape[0]
    ch = math.gcd(GATHER_CHUNK, m)
    return pl.pallas_call(
        _gather_kernel,
        grid=(m // ch,),
        in_specs=[pl.BlockSpec(memory_space=pl.ANY), pl.BlockSpec(memory_space=pl.ANY)],
        out_specs=pl.BlockSpec(memory_space=pl.ANY),
        out_shape=jax.ShapeDtypeStruct((m, src.shape[1]), src.dtype),
        scratch_shapes=[pltpu.SMEM((2, ch), jnp.int32),
                        pltpu.SemaphoreType.DMA((2,)),
                        pltpu.SemaphoreType.DMA((2,))],
        compiler_params=_cp(("arbitrary",)),
        name="row_gather",
    )(idx.reshape(m // ch, ch), src)


def _expert_kernel(be_ref, x_ref, gate_ref, wu_ref, bu_ref, wd_ref, bd_ref, o_ref):
    x = x_ref[...].astype(BF)
    h = _dot(x, wu_ref[...]) + bu_ref[...]
    hg = jnp.minimum(h[:, :D_FF], SWIGLU_LIMIT)
    hl = jnp.clip(h[:, D_FF:], -SWIGLU_LIMIT, SWIGLU_LIMIT)
    act = hg * _sigmoid(SWIGLU_ALPHA * hg) * (hl + 1.0)
    y = _dot(act.astype(BF), wd_ref[...]) + bd_ref[...]
    o_ref[...] = y * gate_ref[...]


def _experts(xs, row_gate, block_e, w_up, b_up, w_down, b_down):
    n_rows = xs.shape[0]
    bm = MOE_ROWS
    grid_spec = pltpu.PrefetchScalarGridSpec(
        num_scalar_prefetch=1,
        grid=(n_rows // bm,),
        in_specs=[pl.BlockSpec((bm, D_MODEL), lambda i, be: (i, 0)),
                  pl.BlockSpec((bm, 1), lambda i, be: (i, 0)),
                  pl.BlockSpec((None, D_MODEL, 2 * D_FF), lambda i, be: (be[i], 0, 0)),
                  pl.BlockSpec((None, 1, 2 * D_FF), lambda i, be: (be[i], 0, 0)),
                  pl.BlockSpec((None, D_FF, D_MODEL), lambda i, be: (be[i], 0, 0)),
                  pl.BlockSpec((None, 1, D_MODEL), lambda i, be: (be[i], 0, 0))],
        out_specs=pl.BlockSpec((bm, D_MODEL), lambda i, be: (i, 0)))
    return pl.pallas_call(
        _expert_kernel,
        grid_spec=grid_spec,
        out_shape=jax.ShapeDtypeStruct((n_rows, D_MODEL), F32),
        compiler_params=_cp(("arbitrary",)),
        name="experts",
    )(block_e, xs, row_gate[:, None], w_up, b_up[:, None, :], w_down, b_down[:, None, :])


def _combine_kernel(h_ref, g_ref, o_ref):
    acc = h_ref[...]
    for kk in range(TOP_K):
        acc = acc + g_ref[kk]
    o_ref[...] = acc


def _combine(h, gathered, tm):
    n = h.shape[0]
    return pl.pallas_call(
        _combine_kernel,
        grid=(n // tm,),
        in_specs=[pl.BlockSpec((tm, D_MODEL), lambda i: (i, 0)),
                  pl.BlockSpec((TOP_K, tm, D_MODEL), lambda i: (0, i, 0))],
        out_specs=pl.BlockSpec((tm, D_MODEL), lambda i: (i, 0)),
        out_shape=jax.ShapeDtypeStruct((n, D_MODEL), F32),
        compiler_params=_cp(("parallel",)),
        name="combine",
    )(h, gathered)


def _route(top_e, top_g):
    n = top_e.shape[1]
    nk = n * TOP_K
    bm = MOE_ROWS
    n_blocks = -(-(nk + N_EXPERTS * (bm - 1)) // bm)
    n_rows = n_blocks * bm
    flat_e = top_e.reshape(-1)
    order = jnp.argsort(flat_e)
    sorted_e = flat_e[order]
    counts = jnp.bincount(flat_e, length=N_EXPERTS)
    padded = (counts + bm - 1) // bm * bm
    start = jnp.cumsum(counts) - counts
    pend = jnp.cumsum(padded)
    pstart = pend - padded
    dest = (pstart[sorted_e] + jnp.arange(nk) - start[sorted_e]).astype(jnp.int32)
    row_tok = jnp.zeros((n_rows,), jnp.int32).at[dest].set((order % n).astype(jnp.int32))
    row_gate = jnp.zeros((n_rows,), F32).at[dest].set(top_g.reshape(-1)[order])
    slot_row = jnp.zeros((nk,), jnp.int32).at[order].set(dest)
    bstart = jnp.arange(n_blocks) * bm
    block_e = jnp.minimum(jnp.sum(pend[None, :] <= bstart[:, None], axis=1),
                          N_EXPERTS - 1).astype(jnp.int32)
    return row_tok, row_gate, slot_row, block_e


def _moe(h, hn, top_e, top_g, wts, tm):
    n = h.shape[0]
    row_tok, row_gate, slot_row, block_e = _route(top_e, top_g)
    xs = _row_gather(hn, row_tok)
    ys = _experts(xs, row_gate, block_e, *wts)
    gathered = _row_gather(ys, slot_row).reshape(TOP_K, n, D_MODEL)
    return _combine(h, gathered, tm)


def _pick(n, pref):
    t = min(n, pref)
    assert n % t == 0
    return t


def _layer(x, pos, s0, shift0, p, w_in_bf, moe_wts, attend, chunk):
    b, t, _ = x.shape
    n = b * t
    tm = _pick(n, 256)
    tt = _pick(t, 256)
    x2 = x.reshape(n, D_MODEL)
    zr, zq, zk, zv, zvb, zg = _inproj(x2, p["norm1_g"][None, :], w_in_bf, tm)
    zr3 = zr.reshape(b, t, RW_IN)
    q, k, kb = _qk_prep(zq.reshape(b, t, DA_WIDTH), zk.reshape(b, t, DA_WIDTH),
                        p["q_norm_g"], p["k_norm_g"], pos, tt)
    r, lw, km, v, a, bb, g = _rwkv_prep(zr3, shift0, p, tt)
    ya, s_t = _rwkv_chunk(r, lw, km, v, a, bb, g, s0, p, chunk)
    ob = attend(q, kb, zvb.reshape(b, t, DA_WIDTH))
    h, hn, top_e, top_g = _merge(ya.reshape(n, RW_WIDTH), ob.reshape(n, DA_WIDTH), zg, x2, p, tm)
    y = _moe(h, hn, top_e, top_g, moe_wts, tm)
    return (y.reshape(b, t, D_MODEL), k.reshape(b, t, DA_HEADS, 2 * DA_HEAD_DIM),
            zv.reshape(b, t, DA_HEADS, DA_V_DIM), s_t, zr3[:, -1])


def kernel(x_prompt, x_sample, cache_k, cache_v, page_table, state_wkv, state_shift, norm1_g, w_in, mu_shift, w0, w_w2, a0, a_a2, g_g2, k_k, k_a, r_k, ln_x_w, ln_x_b, q_norm_g, k_norm_g, lambda_q1, lambda_k1, lambda_q2, lambda_k2, subln_g, w_a, w_b, w_o, norm2_g, w_router, b_router, w_up, b_up, w_down, b_down):
    names = ["norm1_g", "mu_shift", "w0", "w_w2", "a0", "a_a2", "g_g2", "k_k", "k_a", "r_k",
             "ln_x_w", "ln_x_b", "q_norm_g", "k_norm_g", "lambda_q1", "lambda_k1", "lambda_q2",
             "lambda_k2", "subln_g", "w_a", "w_b", "w_o", "norm2_g", "w_router", "b_router"]
    vals = [norm1_g, mu_shift, w0, w_w2, a0, a_a2, g_g2, k_k, k_a, r_k, ln_x_w, ln_x_b,
            q_norm_g, k_norm_g, lambda_q1, lambda_k1, lambda_q2, lambda_k2, subln_g,
            w_a, w_b, w_o, norm2_g, w_router, b_router]
    p = {nm: vv[0] for nm, vv in zip(names, vals)}
    w_in_bf = w_in[0].astype(BF)
    moe_wts = (w_up[0].astype(BF), b_up[0], w_down[0].astype(BF), b_down[0])

    bp, tp, _ = x_prompt.shape
    bs, ts, _ = x_sample.shape
    past = page_table.shape[1] * PAGE_SIZE

    attend_p = lambda q, kb, vb: _prompt_attention(q, kb, vb, p, _pick(tp, 512))
    yp, kp, vp, wp, sp = _layer(
        x_prompt, jnp.arange(tp), jnp.zeros((bp, RW_HEADS, RW_HEAD_DIM, RW_HEAD_DIM), F32),
        jnp.zeros((bp, RW_IN), F32), p, w_in_bf, moe_wts, attend_p, _pick(tp, 64))

    attend_s = lambda q, kb, vb: _sample_attention(
        q, kb, vb, cache_k[0], cache_v[0], page_table, p, _pick(page_table.shape[1], 8))
    ys, ks, vs, ws, ss = _layer(
        x_sample, past + jnp.arange(ts), state_wkv[0], state_shift[0], p, w_in_bf, moe_wts,
        attend_s, ts)

    return (yp, ys, kp[None], vp[None], wp[None], sp[None],
            ks[None], vs[None], ws[None], ss[None])
```

```python
import functools
import math

import jax
import jax.numpy as jnp
import numpy as np
from jax import lax
from jax.experimental import pallas as pl
from jax.experimental.pallas import tpu as pltpu

F32 = jnp.float32
BF = jnp.bfloat16

D_MODEL = 1024
PAGE_SIZE = 128
RW_HEADS = 8
RW_HEAD_DIM = 64
RW_WIDTH = RW_HEADS * RW_HEAD_DIM
DECAY_LORA = 64
AAA_LORA = 64
GATE_LORA = 128
RW_IN = 3 * RW_WIDTH + DECAY_LORA + AAA_LORA + GATE_LORA
GN_EPS = 64e-5
DA_HEADS = 4
DA_HEAD_DIM = 64
DA_V_DIM = 2 * DA_HEAD_DIM
DA_WIDTH = DA_HEADS * DA_V_DIM
ROPE_DIM = DA_HEAD_DIM // 4
ROPE_THETA = 500000.0
N_IN = RW_IN + 3 * DA_WIDTH + 2 * D_MODEL
N_EXPERTS = 32
TOP_K = 4
D_FF = D_MODEL
SWIGLU_ALPHA = 1.702
SWIGLU_LIMIT = 7.0
NORM_EPS = 1e-5
LAMBDA_INIT = 0.8 - 0.6 * math.exp(-0.3 * 0)

VMEM_LIMIT = 56 * 1024 * 1024
MOE_ROWS = 256
ROW_TILES = D_MODEL // 128
assert ROW_TILES == 8

_NT = (((1,), (1,)), ((), ()))
_TN = (((0,), (0,)), ((), ()))


def _cp(sem, vmem=VMEM_LIMIT):
    return pltpu.CompilerParams(dimension_semantics=sem, vmem_limit_bytes=vmem)


def _dot(a, b):
    return jnp.dot(a, b, preferred_element_type=F32)


def _dg(a, b, dims):
    return lax.dot_general(a, b, dims, preferred_element_type=F32)


def _split(x):
    hi = x.astype(BF)
    lo = (x - hi.astype(F32)).astype(BF)
    return hi, lo


def _dot_exact_rhs(a, b_bf16):
    ah, al = _split(a)
    return _dot(ah, b_bf16) + _dot(al, b_bf16)


def _sigmoid(x):
    return 1.0 / (1.0 + jnp.exp(-x))


_SEGS = (0, RW_IN, RW_IN + DA_WIDTH, RW_IN + 2 * DA_WIDTH, RW_IN + 3 * DA_WIDTH, N_IN)


def _inproj_kernel(x_ref, g_ref, w_ref, zr_ref, zq_ref, zk_ref, zv_ref, zvb_ref, zg_ref):
    x = x_ref[...]
    xn = x * lax.rsqrt(jnp.mean(x * x, axis=-1, keepdims=True) + NORM_EPS) * g_ref[...]
    xb = xn.astype(BF)
    zr_ref[...] = _dot(xb, w_ref[:, _SEGS[0]:_SEGS[1]])
    zq_ref[...] = _dot(xb, w_ref[:, _SEGS[1]:_SEGS[2]])
    zk_ref[...] = _dot(xb, w_ref[:, _SEGS[2]:_SEGS[3]])
    zv = _dot(xb, w_ref[:, _SEGS[3]:_SEGS[4]])
    zv_ref[...] = zv
    zvb_ref[...] = zv.astype(BF)
    zg_ref[...] = _dot(xb, w_ref[:, _SEGS[4]:_SEGS[5]])


def _inproj(x2, g, w_bf, tm):
    n = x2.shape[0]
    row = lambda w: pl.BlockSpec((tm, w), lambda i: (i, 0))
    return pl.pallas_call(
        _inproj_kernel,
        grid=(n // tm,),
        in_specs=[row(D_MODEL),
                  pl.BlockSpec((1, D_MODEL), lambda i: (0, 0)),
                  pl.BlockSpec((D_MODEL, N_IN), lambda i: (0, 0))],
        out_specs=[row(RW_IN), row(DA_WIDTH), row(DA_WIDTH), row(DA_WIDTH), row(DA_WIDTH),
                   row(2 * D_MODEL)],
        out_shape=[jax.ShapeDtypeStruct((n, RW_IN), F32),
                   jax.ShapeDtypeStruct((n, DA_WIDTH), F32),
                   jax.ShapeDtypeStruct((n, DA_WIDTH), F32),
                   jax.ShapeDtypeStruct((n, DA_WIDTH), F32),
                   jax.ShapeDtypeStruct((n, DA_WIDTH), BF),
                   jax.ShapeDtypeStruct((n, 2 * D_MODEL), F32)],
        compiler_params=_cp(("parallel",)),
        name="inproj",
    )(x2, g, w_bf)


def _group_ones(width, group):
    i = np.arange(width)
    return jnp.asarray((i[:, None] // group) == (i[None, :] // group), dtype=BF)


def _qk_prep_kernel(zq_ref, zk_ref, gq_ref, gk_ref, cos_ref, sa_ref, sb_ref, j_ref,
                    q_ref, k_ref, kb_ref):
    cos = jnp.concatenate([cos_ref[...]] * (DA_WIDTH // 128), axis=-1)
    sa = jnp.concatenate([sa_ref[...]] * (DA_WIDTH // 128), axis=-1)
    sb = jnp.concatenate([sb_ref[...]] * (DA_WIDTH // 128), axis=-1)

    def norm_rope(z, g):
        ms = _dot_exact_rhs(z * z, j_ref[...]) * (1.0 / DA_HEAD_DIM)
        y = z * lax.rsqrt(ms + NORM_EPS) * g
        half = ROPE_DIM // 2
        up = pltpu.roll(y, DA_WIDTH - half, axis=1)
        dn = pltpu.roll(y, half, axis=1)
        return y * cos + up * sa + dn * sb

    q = norm_rope(zq_ref[0], gq_ref[...])
    k = norm_rope(zk_ref[0], gk_ref[...])
    q_ref[0] = (q * (DA_HEAD_DIM ** -0.5)).astype(BF)
    k_ref[0] = k
    kb_ref[0] = k.astype(BF)


def _rope_tables(pos):
    half = ROPE_DIM // 2
    inv = ROPE_THETA ** (-jnp.arange(half, dtype=F32) / half)
    ang = pos.astype(F32)[:, None] * inv[None, :]
    lane = np.arange(128) % DA_HEAD_DIM
    idx = lane % half
    c = jnp.cos(ang)[:, idx]
    s = jnp.sin(ang)[:, idx]
    in_rope = jnp.asarray(lane < ROPE_DIM)
    first = jnp.asarray(lane < half)
    second = jnp.asarray((lane >= half) & (lane < ROPE_DIM))
    cos_t = jnp.where(in_rope[None, :], c, 1.0)
    sa_t = jnp.where(first[None, :], -s, 0.0)
    sb_t = jnp.where(second[None, :], s, 0.0)
    return cos_t, sa_t, sb_t


def _qk_prep(zq, zk, gq, gk, pos, tt):
    b, t, _ = zq.shape
    cos_t, sa_t, sb_t = _rope_tables(pos)
    gq_t = jnp.tile(gq, DA_WIDTH // DA_HEAD_DIM)[None, :]
    gk_t = jnp.tile(gk, DA_WIDTH // DA_HEAD_DIM)[None, :]
    tok = pl.BlockSpec((1, tt, DA_WIDTH), lambda i, j: (i, j, 0))
    tab = pl.BlockSpec((tt, 128), lambda i, j: (j, 0))
    par = pl.BlockSpec((1, DA_WIDTH), lambda i, j: (0, 0))
    return pl.pallas_call(
        _qk_prep_kernel,
        grid=(b, t // tt),
        in_specs=[tok, tok, par, par, tab, tab, tab,
                  pl.BlockSpec((DA_WIDTH, DA_WIDTH), lambda i, j: (0, 0))],
        out_specs=[tok, tok, tok],
        out_shape=[jax.ShapeDtypeStruct((b, t, DA_WIDTH), BF),
                   jax.ShapeDtypeStruct((b, t, DA_WIDTH), F32),
                   jax.ShapeDtypeStruct((b, t, DA_WIDTH), BF)],
        compiler_params=_cp(("parallel", "parallel")),
        name="qk_prep",
    )(zq, zk, gq_t, gk_t, cos_t, sa_t, sb_t, _group_ones(DA_WIDTH, DA_HEAD_DIM))


def _rwkv_prep_kernel(zr_ref, sh_ref, mu_ref, w0_ref, a0_ref, kk_ref, ka_ref,
                      w2_ref, a2_ref, g2_ref, j_ref,
                      r_ref, lw_ref, k_ref, v_ref, a_ref, b_ref, g_ref, carry_ref):
    tt = zr_ref.shape[1]

    @pl.when(pl.program_id(1) == 0)
    def _():
        carry_ref[...] = sh_ref[0]

    z = zr_ref[0]
    rolled = pltpu.roll(z, 1, axis=0)
    first_row = lax.broadcasted_iota(jnp.int32, (tt, 1), 0) == 0
    zprev = jnp.where(first_row, carry_ref[...], rolled)
    carry_ref[...] = z[tt - 1:tt, :]
    zs = z + (zprev - z) * mu_ref[...]

    r = zs[:, 0:RW_WIDTH]
    k = zs[:, RW_WIDTH:2 * RW_WIDTH]
    v = zs[:, 2 * RW_WIDTH:3 * RW_WIDTH]
    wa = zs[:, 3 * RW_WIDTH:3 * RW_WIDTH + DECAY_LORA + AAA_LORA]
    gd = zs[:, 3 * RW_WIDTH + DECAY_LORA + AAA_LORA:]

    lw = w0_ref[...] + _dot(jnp.tanh(wa).astype(BF), w2_ref[...])
    log_decay = -math.exp(-0.5) * _sigmoid(lw)
    a = _sigmoid(a0_ref[...] + _dot(wa.astype(BF), a2_ref[...]))
    g = _dot(_sigmoid(gd).astype(BF), g2_ref[...])
    kk = k * kk_ref[...]
    ss = _dot_exact_rhs(kk * kk, j_ref[...])
    kk = kk / jnp.maximum(jnp.sqrt(ss), 1e-12)
    kmod = k * (1.0 + (a - 1.0) * ka_ref[...])

    r_ref[0] = r
    lw_ref[0] = log_decay
    k_ref[0] = kmod
    v_ref[0] = v
    a_ref[0] = -kk
    b_ref[0] = kk * a
    g_ref[0] = g


def _rwkv_prep(zr, shift0, p, tt):
    b, t, _ = zr.shape
    zero_pad = jnp.zeros((DECAY_LORA, RW_WIDTH), F32)
    w2p = jnp.concatenate([p["w_w2"], zero_pad], axis=0).astype(BF)
    a2p = jnp.concatenate([zero_pad, p["a_a2"]], axis=0).astype(BF)
    tok = lambda w: pl.BlockSpec((1, tt, w), lambda i, j: (i, j, 0))
    par = lambda w: pl.BlockSpec((1, w), lambda i, j: (0, 0))
    mat = lambda r, c: pl.BlockSpec((r, c), lambda i, j: (0, 0))
    return pl.pallas_call(
        _rwkv_prep_kernel,
        grid=(b, t // tt),
        in_specs=[tok(RW_IN), pl.BlockSpec((1, 1, RW_IN), lambda i, j: (i, 0, 0)),
                  par(RW_IN), par(RW_WIDTH), par(RW_WIDTH), par(RW_WIDTH), par(RW_WIDTH),
                  mat(DECAY_LORA + AAA_LORA, RW_WIDTH), mat(DECAY_LORA + AAA_LORA, RW_WIDTH),
                  mat(GATE_LORA, RW_WIDTH), mat(RW_WIDTH, RW_WIDTH)],
        out_specs=[tok(RW_WIDTH)] * 7,
        out_shape=[jax.ShapeDtypeStruct((b, t, RW_WIDTH), F32)] * 7,
        scratch_shapes=[pltpu.VMEM((1, RW_IN), F32)],
        compiler_params=_cp(("parallel", "arbitrary")),
        name="rwkv_prep",
    )(zr, shift0[:, None, :], p["mu_shift"][None, :], p["w0"][None, :], p["a0"][None, :],
      p["k_k"][None, :], p["k_a"][None, :], w2p, a2p, p["g_g2"].astype(BF),
      _group_ones(RW_WIDTH, RW_HEAD_DIM))


_NN = (((1,), (0,)), ((), ()))


def _d3(a, b, dims=_NN):
    return _dg(a[0], b[0], dims) + _dg(a[0], b[1], dims) + _dg(a[1], b[0], dims)


def _d1(a, b, dims=_NN):
    return _dg(a[0], b[0], dims)


def _rwkv_chunk_kernel(r_ref, lw_ref, k_ref, v_ref, a_ref, b_ref, g_ref, s0_ref,
                       lnw_ref, lnb_ref, rk_ref, y_ref, sT_ref, s_ref):
    c = r_ref.shape[1]
    n = RW_HEAD_DIM

    @pl.when(pl.program_id(1) == 0)
    def _():
        s_ref[...] = s0_ref[0]

    r = r_ref[0]
    lw = lw_ref[0]
    k = k_ref[0]
    v = v_ref[0]
    a = a_ref[0]
    b = b_ref[0]

    heads = range(RW_HEADS)
    row = lax.broadcasted_iota(jnp.int32, (c, c), 0)
    col = lax.broadcasted_iota(jnp.int32, (c, c), 1)
    lower = row >= col
    strict = row > col
    eye_c = (row == col).astype(F32)
    cum = _d3(_split(lower.astype(F32)), _split(lw))
    cum_last = cum[c - 1:c, :]
    pinv = jnp.exp(-cum)
    pc = jnp.exp(cum_last - cum)
    p_chunk = jnp.exp(cum_last)
    full = {"rt": r * jnp.exp(cum), "at": a * jnp.exp(cum - lw), "bt": b * pinv,
            "kt": k * pinv, "bh": b * pc, "kh": k * pc, "v": v, "rk": r * k * rk_ref[...]}
    per_head = {nm: [x[:, h * n:(h + 1) * n] for h in heads] for nm, x in full.items()}
    at, rt, bt, kt, bh, kh, vv = (
        [_split(x) for x in per_head[nm]] for nm in ("at", "rt", "bt", "kt", "bh", "kh", "v"))

    a_ab = [jnp.where(strict, _d3(at[h], bt[h], _NT), 0.0) for h in heads]
    a_ak = [jnp.where(strict, _d3(at[h], kt[h], _NT), 0.0) for h in heads]
    a_rb = [_split(jnp.where(lower, _d1(rt[h], bt[h], _NT), 0.0)) for h in heads]
    a_rk = [_split(jnp.where(lower, _d1(rt[h], kt[h], _NT), 0.0)) for h in heads]

    t_inv = [eye_c + x for x in a_ab]
    pw = a_ab
    for _ in range(int(math.log2(c)) - 1):
        pw_s = [_split(x) for x in pw]
        pw = [_d3(x, x) for x in pw_s]
        pw_s = [_split(x) for x in pw]
        t_s = [_split(x) for x in t_inv]
        t_inv = [t_inv[h] + _d3(t_s[h], pw_s[h]) for h in heads]
    t_s = [_split(x) for x in t_inv]
    akv = [_split(_d3(_split(a_ak[h]), vv[h])) for h in heads]
    w1 = [_split(_d3(t_s[h], at[h])) for h in heads]
    u0 = [_split(_d3(t_s[h], akv[h])) for h in heads]

    krow = lax.broadcasted_iota(jnp.int32, (n, n), 0)
    kcol = lax.broadcasted_iota(jnp.int32, (n, n), 1)
    eye_n = krow == kcol
    m_mat = [_split(jnp.where(eye_n, p_chunk[:, h * n:(h + 1) * n], 0.0) + _d3(w1[h], bh[h], _TN))
             for h in heads]
    n_mat = [_d3(u0[h], bh[h], _TN) + _d3(vv[h], kh[h], _TN) for h in heads]
    g_mat = [_split(per_head["rt"][h] + _d1(a_rb[h], w1[h])) for h in heads]
    y0 = [_d1(a_rb[h], u0[h]) + _d1(a_rk[h], vv[h]) for h in heads]
    s_old = [_split(s_ref[h]) for h in heads]
    ys = [_d1(g_mat[h], s_old[h], _NT) + y0[h] for h in heads]
    for h in heads:
        s_ref[h] = _d3(s_old[h], m_mat[h]) + n_mat[h]

    yn, bonus = [], []
    for h in heads:
        y = ys[h]
        mu = jnp.mean(y, axis=-1, keepdims=True)
        var = jnp.mean(jnp.square(y - mu), axis=-1, keepdims=True)
        yn.append((y - mu) * lax.rsqrt(var + GN_EPS))
        bonus.append(jnp.sum(per_head["rk"][h], axis=-1, keepdims=True) * per_head["v"][h])
    yn = jnp.concatenate(yn, axis=-1)
    bonus = jnp.concatenate(bonus, axis=-1)
    y_ref[0] = ((yn * lnw_ref[...] + lnb_ref[...] + bonus) * g_ref[0]).astype(BF)

    @pl.when(pl.program_id(1) == pl.num_programs(1) - 1)
    def _():
        sT_ref[0] = s_ref[...]


def _rwkv_chunk(r, lw, k, v, a, b, g, s0, p, c):
    bsz, t, _ = r.shape
    tok = pl.BlockSpec((1, c, RW_WIDTH), lambda i, j: (i, j, 0))
    par = pl.BlockSpec((1, RW_WIDTH), lambda i, j: (0, 0))
    st = pl.BlockSpec((1, RW_HEADS, RW_HEAD_DIM, RW_HEAD_DIM), lambda i, j: (i, 0, 0, 0))
    return pl.pallas_call(
        _rwkv_chunk_kernel,
        grid=(bsz, t // c),
        in_specs=[tok] * 7 + [st, par, par, par],
        out_specs=[tok, st],
        out_shape=[jax.ShapeDtypeStruct((bsz, t, RW_WIDTH), BF),
                   jax.ShapeDtypeStruct((bsz, RW_HEADS, RW_HEAD_DIM, RW_HEAD_DIM), F32)],
        scratch_shapes=[pltpu.VMEM((RW_HEADS, RW_HEAD_DIM, RW_HEAD_DIM), F32)],
        compiler_params=_cp(("parallel", "arbitrary")),
        name="rwkv_chunk",
    )(r, lw, k, v, a, b, g, s0, p["ln_x_w"][None, :], p["ln_x_b"][None, :],
      p["r_k"].reshape(1, RW_WIDTH))


def _lambda_full(lq1, lk1, lq2, lk2):
    s1 = jnp.sum(lq1 * lk1, axis=-1, keepdims=True)
    s2 = jnp.sum(lq2 * lk2, axis=-1, keepdims=True)
    return jnp.exp(s1) - jnp.exp(s2) + LAMBDA_INIT


def _subln(o, g):
    y = o * lax.rsqrt(jnp.mean(o * o, axis=-1, keepdims=True) + NORM_EPS)
    return y * g * (1.0 - LAMBDA_INIT)


def _prompt_attn_kernel(qi_ref, ki_ref, q_ref, k_ref, v_ref, lq1, lk1, lq2, lk2, sg_ref,
                        o_ref, qm_ref, m_ref, l_ref, acc_ref):
    p = pl.program_id(2)
    qi = qi_ref[p]
    ki = ki_ref[p]
    tq = q_ref.shape[1]
    tk = k_ref.shape[1]

    @pl.when(ki == 0)
    def _():
        q = q_ref[0]
        lane = lax.broadcasted_iota(jnp.int32, q.shape, 1)
        zero = jnp.zeros_like(q)
        qm_ref[0:tq] = jnp.where(lane < DA_HEAD_DIM, q, zero)
        qm_ref[tq:2 * tq] = jnp.where(lane >= DA_HEAD_DIM, q, zero)
        m_ref[...] = jnp.full(m_ref.shape, -jnp.inf, F32)
        l_ref[...] = jnp.zeros(l_ref.shape, F32)
        acc_ref[...] = jnp.zeros(acc_ref.shape, F32)

    def step(masked):
        s = _dg(qm_ref[...], k_ref[0], _NT)
        if masked:
            rowi = lax.broadcasted_iota(jnp.int32, (2 * tq, tk), 0)
            coli = lax.broadcasted_iota(jnp.int32, (2 * tq, tk), 1)
            s = jnp.where(coli <= jnp.where(rowi >= tq, rowi - tq, rowi), s, -jnp.inf)
        m_old = m_ref[...]
        m_new = jnp.maximum(m_old, jnp.max(s, axis=-1, keepdims=True))
        alpha = jnp.exp(m_old - m_new)
        pr = jnp.exp(s - jnp.concatenate([m_new] * (tk // 128), axis=-1))
        l_ref[...] = alpha * l_ref[...] + jnp.sum(pr, axis=-1, keepdims=True)
        acc_ref[...] = alpha * acc_ref[...] + _dot(pr.astype(BF), v_ref[0])
        m_ref[...] = m_new

    @pl.when(ki < qi)
    def _():
        step(False)

    @pl.when(ki == qi)
    def _():
        step(True)
        lam = _lambda_full(lq1[...], lk1[...], lq2[...], lk2[...])
        o = acc_ref[...] / l_ref[...]
        o_ref[0] = _subln(o[0:tq] - lam * o[tq:2 * tq], sg_ref[...]).astype(BF)


def _prompt_attention(q, kb, vb, p, tq):
    bsz, t, _ = q.shape
    nq = t // tq
    qi = np.concatenate([np.full(i + 1, i) for i in range(nq)]).astype(np.int32)
    ki = np.concatenate([np.arange(i + 1) for i in range(nq)]).astype(np.int32)
    lam_spec = pl.BlockSpec((1, DA_HEAD_DIM), lambda b, h, s, qt, kt: (0, 0))
    grid_spec = pltpu.PrefetchScalarGridSpec(
        num_scalar_prefetch=2,
        grid=(bsz, DA_HEADS, len(qi)),
        in_specs=[pl.BlockSpec((1, tq, DA_V_DIM), lambda b, h, s, qt, kt: (b, qt[s], h)),
                  pl.BlockSpec((1, tq, DA_V_DIM), lambda b, h, s, qt, kt: (b, kt[s], h)),
                  pl.BlockSpec((1, tq, DA_V_DIM), lambda b, h, s, qt, kt: (b, kt[s], h)),
                  lam_spec, lam_spec, lam_spec, lam_spec,
                  pl.BlockSpec((1, DA_V_DIM), lambda b, h, s, qt, kt: (0, 0))],
        out_specs=pl.BlockSpec((1, tq, DA_V_DIM), lambda b, h, s, qt, kt: (b, qt[s], h)),
        scratch_shapes=[pltpu.VMEM((2 * tq, DA_V_DIM), BF),
                        pltpu.VMEM((2 * tq, 128), F32),
                        pltpu.VMEM((2 * tq, 128), F32),
                        pltpu.VMEM((2 * tq, DA_V_DIM), F32)])
    return pl.pallas_call(
        _prompt_attn_kernel,
        grid_spec=grid_spec,
        out_shape=jax.ShapeDtypeStruct((bsz, t, DA_WIDTH), BF),
        compiler_params=_cp(("parallel", "parallel", "arbitrary")),
        name="prompt_attn",
    )(jnp.asarray(qi), jnp.asarray(ki), q, kb, vb,
      p["lambda_q1"][None, :], p["lambda_k1"][None, :], p["lambda_q2"][None, :],
      p["lambda_k2"][None, :], p["subln_g"][None, :])


def _make_sample_attn_kernel(pages):
    n_pairs = DA_HEADS * 2

    def kernel(pt_ref, q_ref, kn_ref, vn_ref, lq1, lk1, lq2, lk2, sg_ref, *rest):
        k_refs = rest[:pages]
        v_refs = rest[pages:2 * pages]
        o_ref = rest[2 * pages]
        qall_ref, m_ref, l_ref, acc_ref = rest[2 * pages + 1:]
        j = pl.program_id(1)
        s_len = q_ref.shape[1]
        rows = n_pairs * s_len
        page_rows = PAGE_SIZE * DA_HEADS

        @pl.when(j == 0)
        def _():
            q = q_ref[0].astype(F32)
            lane = lax.broadcasted_iota(jnp.int32, (s_len, DA_V_DIM), 1)
            pieces = []
            for h in range(DA_HEADS):
                qh = q[:, h * DA_V_DIM:(h + 1) * DA_V_DIM]
                pieces.append(jnp.where(lane < DA_HEAD_DIM, qh, 0.0))
                pieces.append(jnp.where(lane >= DA_HEAD_DIM, qh, 0.0))
            qall_ref[...] = jnp.concatenate(pieces, axis=0).astype(BF)
            m_ref[...] = jnp.full(m_ref.shape, -jnp.inf, F32)
            l_ref[...] = jnp.zeros(l_ref.shape, F32)
            acc_ref[...] = jnp.zeros(acc_ref.shape, F32)

        def update(s, vals):
            m_old = m_ref[...]
            m_new = jnp.maximum(m_old, jnp.max(s, axis=-1, keepdims=True))
            alpha = jnp.exp(m_old - m_new)
            pr = jnp.exp(s - m_new)
            l_ref[...] = alpha * l_ref[...] + jnp.sum(pr, axis=-1, keepdims=True)
            acc_ref[...] = alpha * acc_ref[...] + _dot(pr.astype(BF), vals)
            m_ref[...] = m_new

        qall = qall_ref[...]
        flat = lambda ref: ref[...].reshape(page_rows, DA_V_DIM).astype(BF)
        s_all = jnp.concatenate([_dg(qall, flat(kr), _NT) for kr in k_refs], axis=-1)
        v_all = jnp.concatenate([flat(vr) for vr in v_refs], axis=0)
        row_head = lax.broadcasted_iota(jnp.int32, s_all.shape, 0) // (2 * s_len)
        key_head = lax.broadcasted_iota(jnp.int32, s_all.shape, 1) % DA_HEADS
        update(jnp.where(row_head == key_head, s_all, -jnp.inf), v_all)

        @pl.when(j == pl.num_programs(1) - 1)
        def _():
            kn = kn_ref[0].astype(F32)
            vn = vn_ref[0].astype(F32)
            by_head = lambda x: jnp.concatenate(
                [x[:, h * DA_V_DIM:(h + 1) * DA_V_DIM] for h in range(DA_HEADS)], axis=0).astype(BF)
            s = _dg(qall, by_head(kn), _NT)
            r = lax.broadcasted_iota(jnp.int32, s.shape, 0)
            c = lax.broadcasted_iota(jnp.int32, s.shape, 1)
            keep = (c // s_len == r // (2 * s_len)) & (c % s_len <= r % s_len)
            update(jnp.where(keep, s, -jnp.inf), by_head(vn))
            lam = _lambda_full(lq1[...], lk1[...], lq2[...], lk2[...])
            o = acc_ref[...] / l_ref[...]
            heads = []
            for h in range(DA_HEADS):
                o1 = o[(2 * h) * s_len:(2 * h + 1) * s_len]
                o2 = o[(2 * h + 1) * s_len:(2 * h + 2) * s_len]
                heads.append(_subln(o1 - lam * o2, sg_ref[...]))
            o_ref[0] = jnp.concatenate(heads, axis=-1).astype(BF)

    return kernel


def _sample_attention(q, kb, vb, cache_k, cache_v, page_table, p, pages):
    bsz, s_len, _ = q.shape
    n_pages = page_table.shape[1]
    rows = DA_HEADS * 2 * s_len
    tok = pl.BlockSpec((1, s_len, DA_WIDTH), lambda b, j, pt: (b, 0, 0))
    lam_spec = pl.BlockSpec((1, DA_HEAD_DIM), lambda b, j, pt: (0, 0))

    def page_spec(i):
        return pl.BlockSpec((None, None, PAGE_SIZE, DA_HEADS, DA_V_DIM),
                            lambda b, j, pt: (0, pt[b * n_pages + j * pages + i], 0, 0, 0))

    grid_spec = pltpu.PrefetchScalarGridSpec(
        num_scalar_prefetch=1,
        grid=(bsz, n_pages // pages),
        in_specs=[tok, tok, tok, lam_spec, lam_spec, lam_spec, lam_spec,
                  pl.BlockSpec((1, DA_V_DIM), lambda b, j, pt: (0, 0))]
                 + [page_spec(i) for i in range(pages)] * 2,
        out_specs=tok,
        scratch_shapes=[pltpu.VMEM((rows, DA_V_DIM), BF),
                        pltpu.VMEM((rows, 1), F32),
                        pltpu.VMEM((rows, 1), F32),
                        pltpu.VMEM((rows, DA_V_DIM), F32)])
    return pl.pallas_call(
        _make_sample_attn_kernel(pages),
        grid_spec=grid_spec,
        out_shape=jax.ShapeDtypeStruct((bsz, s_len, DA_WIDTH), BF),
        compiler_params=_cp(("parallel", "arbitrary")),
        name="sample_attn",
    )(page_table.reshape(-1), q, kb, vb,
      p["lambda_q1"][None, :], p["lambda_k1"][None, :], p["lambda_q2"][None, :],
      p["lambda_k2"][None, :], p["subln_g"][None, :], *([cache_k] * pages), *([cache_v] * pages))


def _merge_kernel(ya_ref, ob_ref, zg_ref, x_ref, wa_ref, wb_ref, wo_ref, g2_ref,
                  wr_hi_ref, wr_lo_ref, br_ref, h_ref, hn_ref, te_ref, tg_ref):
    zg = zg_ref[...]
    m = (_sigmoid(zg[:, :D_MODEL]) * _dot(ya_ref[...], wa_ref[...])
         + _sigmoid(zg[:, D_MODEL:]) * _dot(ob_ref[...], wb_ref[...]))
    h = x_ref[...] + _dot(m.astype(BF), wo_ref[...])
    h_ref[...] = h
    hn = h * lax.rsqrt(jnp.mean(h * h, axis=-1, keepdims=True) + NORM_EPS) * g2_ref[...]
    tm = hn.shape[0]
    for j in range(ROW_TILES):
        hn_ref[pl.ds(j, tm, stride=ROW_TILES), :] = hn[:, j * 128:(j + 1) * 128]

    hh, hl = _split(hn)
    logits = (_dg(wr_hi_ref[...], hh, _NT) + _dg(wr_hi_ref[...], hl, _NT)
              + _dg(wr_lo_ref[...], hh, _NT)) + br_ref[...]
    eidx = lax.broadcasted_iota(jnp.int32, logits.shape, 0)
    vals, idxs = [], []
    for _ in range(TOP_K):
        mx = jnp.max(logits, axis=0, keepdims=True)
        am = jnp.min(jnp.where(logits == mx, eidx, N_EXPERTS), axis=0, keepdims=True)
        vals.append(mx)
        idxs.append(am)
        logits = jnp.where(eidx == am, -jnp.inf, logits)
    ex = [jnp.exp(vv - vals[0]) for vv in vals]
    den = ex[0] + ex[1] + ex[2] + ex[3]
    te_ref[...] = jnp.concatenate(idxs, axis=0)
    tg_ref[...] = jnp.concatenate([e / den for e in ex], axis=0)


def _merge(ya, ob, zg, x2, p, tm):
    n = x2.shape[0]
    wr_t = p["w_router"].T
    wr_hi = wr_t.astype(BF)
    wr_lo = (wr_t - wr_hi.astype(F32)).astype(BF)
    row = lambda w: pl.BlockSpec((tm, w), lambda i: (i, 0))
    mat = lambda r, c: pl.BlockSpec((r, c), lambda i: (0, 0))
    colblk = pl.BlockSpec((TOP_K, tm), lambda i: (0, i))
    return pl.pallas_call(
        _merge_kernel,
        grid=(n // tm,),
        in_specs=[row(RW_WIDTH), row(DA_WIDTH), row(2 * D_MODEL), row(D_MODEL),
                  mat(RW_WIDTH, D_MODEL), mat(DA_WIDTH, D_MODEL), mat(D_MODEL, D_MODEL),
                  mat(1, D_MODEL), mat(N_EXPERTS, D_MODEL), mat(N_EXPERTS, D_MODEL),
                  mat(N_EXPERTS, 1)],
        out_specs=[row(D_MODEL), pl.BlockSpec((tm * ROW_TILES, 128), lambda i: (i, 0)),
                   colblk, colblk],
        out_shape=[jax.ShapeDtypeStruct((n, D_MODEL), F32),
                   jax.ShapeDtypeStruct((n * ROW_TILES, 128), F32),
                   jax.ShapeDtypeStruct((TOP_K, n), jnp.int32),
                   jax.ShapeDtypeStruct((TOP_K, n), F32)],
        compiler_params=_cp(("parallel",)),
        name="merge_router",
    )(ya, ob, zg, x2, p["w_a"].astype(BF), p["w_b"].astype(BF), p["w_o"].astype(BF),
      p["norm2_g"][None, :], wr_hi, wr_lo, p["b_router"][:, None])


def _experts_kernel(ib_ref, ie_ref, ilo_ref, ihi_ref, ifirst_ref, ilast_ref,
                    idx_hbm, x_hbm, wu_ref, bu_ref, wd_ref, bd_ref, out_hbm,
                    idx_smem, idx_sem, xbuf, gsem, xb_ref, acc_ref, obuf, ssem):
    i = pl.program_id(0)
    bm = MOE_ROWS
    slab = bm * ROW_TILES
    n_blocks = idx_hbm.shape[0]
    blk = ib_ref[i]
    s2 = blk % 2
    s3 = blk % 3

    def idx_copy(b, slot3):
        return pltpu.make_async_copy(idx_hbm.at[b], idx_smem.at[slot3], idx_sem.at[slot3])

    def tile_rows(row):
        return pl.ds(pl.multiple_of(row * ROW_TILES, ROW_TILES), ROW_TILES)

    def gather_rows(slot3, slot2):
        def body(r, carry):
            pltpu.make_async_copy(x_hbm.at[tile_rows(idx_smem[slot3, 0, r])],
                                  xbuf.at[tile_rows(slot2 * bm + r)], gsem.at[slot2]).start()
            return carry
        lax.fori_loop(0, bm, body, 0, unroll=8)

    def scatter_rows(slot3, slot2):
        def body(r, carry):
            pltpu.make_async_copy(obuf.at[tile_rows(slot2 * bm + r)],
                                  out_hbm.at[tile_rows(idx_smem[slot3, 1, r])], ssem.at[slot2]).start()
            return carry
        lax.fori_loop(0, bm, body, 0, unroll=8)

    def block_rows(slot2):
        return pl.ds(pl.multiple_of(slot2 * slab, slab), slab)

    def wait_gather(slot2):
        pltpu.make_async_copy(x_hbm.at[pl.ds(0, slab)], xbuf.at[block_rows(slot2)], gsem.at[slot2]).wait()

    def wait_scatter(slot2):
        pltpu.make_async_copy(obuf.at[block_rows(slot2)], out_hbm.at[pl.ds(0, slab)], ssem.at[slot2]).wait()

    @pl.when(i == 0)
    def _():
        idx_copy(0, 0).start()
        idx_copy(0, 0).wait()
        gather_rows(0, 0)
        if n_blocks > 1:
            idx_copy(1, 1).start()

    @pl.when(ifirst_ref[i] == 1)
    def _():
        wait_gather(s2)

        @pl.when(blk + 1 < n_blocks)
        def _():
            nxt = (blk + 1) % 3
            idx_copy(blk + 1, nxt).wait()
            gather_rows(nxt, 1 - s2)

        @pl.when(blk + 2 < n_blocks)
        def _():
            idx_copy(blk + 2, (blk + 2) % 3).start()

        base = s2 * slab
        x = jnp.concatenate([xbuf[pl.ds(base + j, bm, stride=ROW_TILES), :] for j in range(ROW_TILES)],
                            axis=-1)
        xb_ref[...] = x.astype(BF)
        acc_ref[...] = jnp.zeros(acc_ref.shape, F32)

    lo = ilo_ref[i]
    hi = ihi_ref[i]

    @pl.when(hi > lo)
    def _():
        h = _dot(xb_ref[...], wu_ref[...]) + bu_ref[...]
        hg = jnp.minimum(h[:, :D_FF], SWIGLU_LIMIT)
        hl = jnp.clip(h[:, D_FF:], -SWIGLU_LIMIT, SWIGLU_LIMIT)
        act = hg * _sigmoid(SWIGLU_ALPHA * hg) * (hl + 1.0)
        y = _dot(act.astype(BF), wd_ref[...]) + bd_ref[...]
        rowi = lax.broadcasted_iota(jnp.int32, (bm, 1), 0)
        acc_ref[...] += jnp.where((rowi >= lo) & (rowi < hi), y, 0.0)

    @pl.when(ilast_ref[i] == 1)
    def _():
        @pl.when(blk >= 2)
        def _():
            wait_scatter(s2)

        base = s2 * slab
        for j in range(ROW_TILES):
            obuf[pl.ds(base + j, bm, stride=ROW_TILES), :] = acc_ref[:, j * 128:(j + 1) * 128]
        scatter_rows(s3, s2)

    @pl.when(i == pl.num_programs(0) - 1)
    def _():
        wait_scatter((n_blocks - 1) % 2)
        if n_blocks > 1:
            wait_scatter((n_blocks - 2) % 2)


def _experts(hn_rows, items, idx, w_up, b_up, w_down, b_down):
    bm = MOE_ROWS
    n_blocks = idx.shape[0]
    n_items = items[0].shape[0]
    wspec = lambda r, c: pl.BlockSpec((None, r, c), lambda i, ib, ie, *_: (ie[i], 0, 0))
    grid_spec = pltpu.PrefetchScalarGridSpec(
        num_scalar_prefetch=6,
        grid=(n_items,),
        in_specs=[pl.BlockSpec(memory_space=pl.ANY), pl.BlockSpec(memory_space=pl.ANY),
                  wspec(D_MODEL, 2 * D_FF), wspec(1, 2 * D_FF), wspec(D_FF, D_MODEL), wspec(1, D_MODEL)],
        out_specs=pl.BlockSpec(memory_space=pl.ANY),
        scratch_shapes=[pltpu.SMEM((3, 2, bm), jnp.int32),
                        pltpu.SemaphoreType.DMA((3,)),
                        pltpu.VMEM((2 * bm * ROW_TILES, 128), F32),
                        pltpu.SemaphoreType.DMA((2,)),
                        pltpu.VMEM((bm, D_MODEL), BF),
                        pltpu.VMEM((bm, D_MODEL), F32),
                        pltpu.VMEM((2 * bm * ROW_TILES, 128), F32),
                        pltpu.SemaphoreType.DMA((2,))])
    return pl.pallas_call(
        _experts_kernel,
        grid_spec=grid_spec,
        out_shape=jax.ShapeDtypeStruct((n_blocks * bm * ROW_TILES, 128), F32),
        compiler_params=_cp(("arbitrary",)),
        name="experts",
    )(*items, idx, hn_rows, w_up, b_up[:, None, :], w_down, b_down[:, None, :])


def _combine_kernel(h_ref, g_ref, *rest):
    e_refs = rest[:TOP_K]
    o_ref = rest[TOP_K]
    tm = h_ref.shape[0]
    g = g_ref[...]
    for j in range(ROW_TILES):
        acc = h_ref[:, j * 128:(j + 1) * 128]
        for kk in range(TOP_K):
            acc = acc + g[:, kk:kk + 1] * e_refs[kk][pl.ds(j, tm, stride=ROW_TILES), :]
        o_ref[:, j * 128:(j + 1) * 128] = acc


def _combine(h, gates_t, expert_rows, tm):
    n = h.shape[0]
    nt = n // tm
    slot = lambda kk: pl.BlockSpec((tm * ROW_TILES, 128), lambda i: (kk * nt + i, 0))
    return pl.pallas_call(
        _combine_kernel,
        grid=(nt,),
        in_specs=[pl.BlockSpec((tm, D_MODEL), lambda i: (i, 0)),
                  pl.BlockSpec((tm, TOP_K), lambda i: (i, 0))]
                 + [slot(kk) for kk in range(TOP_K)],
        out_specs=pl.BlockSpec((tm, D_MODEL), lambda i: (i, 0)),
        out_shape=jax.ShapeDtypeStruct((n, D_MODEL), F32),
        compiler_params=_cp(("parallel",)),
        name="combine",
    )(h, gates_t, *([expert_rows] * TOP_K))


def _route(top_e):
    n = top_e.shape[1]
    nk = n * TOP_K
    bm = MOE_ROWS
    n_blocks = nk // bm
    n_items = n_blocks + N_EXPERTS
    flat_e = top_e.reshape(-1)
    _, sorted_slot = lax.sort((flat_e, jnp.arange(nk, dtype=jnp.int32)), num_keys=1)
    experts = jnp.arange(N_EXPERTS, dtype=jnp.int32)
    counts = jnp.sum((flat_e[None, :] == experts[:, None]).astype(jnp.int32), axis=1)
    end = jnp.cumsum(counts)
    start = end - counts
    first_blk = start // bm
    n_e = jnp.where(counts > 0, (end - 1) // bm - first_blk + 1, 0)
    cum = jnp.cumsum(n_e)
    off = cum - n_e
    total = cum[-1]
    it = jnp.arange(n_items, dtype=jnp.int32)
    valid = it < total
    e_of = lambda t: jnp.minimum(jnp.sum((cum[None, :] <= t[:, None]).astype(jnp.int32), axis=1),
                                 N_EXPERTS - 1)
    e_i = jnp.where(valid, e_of(it), e_of(total[None] - 1)[0])
    blk = jnp.where(valid, first_blk[e_i] + it - off[e_i], n_blocks - 1)
    lo = jnp.where(valid, jnp.clip(start[e_i] - blk * bm, 0, bm), 0)
    hi = jnp.where(valid, jnp.clip(end[e_i] - blk * bm, 0, bm), 0)
    prev_blk = jnp.concatenate([jnp.full((1,), -1, jnp.int32), blk[:-1]])
    next_blk = jnp.concatenate([blk[1:], jnp.full((1,), -1, jnp.int32)])
    first = valid & (blk != prev_blk)
    last = valid & ((blk != next_blk) | (it == total - 1))
    items = tuple(x.astype(jnp.int32) for x in (blk, e_i, lo, hi, first, last))
    idx = jnp.stack([(sorted_slot % n).reshape(n_blocks, bm), sorted_slot.reshape(n_blocks, bm)], axis=1)
    return items, idx


def _moe(h, hn_rows, top_e, top_g, wts, tm):
    items, idx = _route(top_e)
    expert_rows = _experts(hn_rows, items, idx, *wts)
    return _combine(h, top_g.T, expert_rows, tm)


def _pick(n, pref):
    t = min(n, pref)
    assert n % t == 0
    return t


def _layer(x, pos, s0, shift0, p, w_in_bf, moe_wts, attend, chunk):
    b, t, _ = x.shape
    n = b * t
    tm = _pick(n, 256)
    tt = _pick(t, 256)
    x2 = x.reshape(n, D_MODEL)
    zr, zq, zk, zv, zvb, zg = _inproj(x2, p["norm1_g"][None, :], w_in_bf, tm)
    zr3 = zr.reshape(b, t, RW_IN)
    q, k, kb = _qk_prep(zq.reshape(b, t, DA_WIDTH), zk.reshape(b, t, DA_WIDTH),
                        p["q_norm_g"], p["k_norm_g"], pos, tt)
    r, lw, km, v, a, bb, g = _rwkv_prep(zr3, shift0, p, tt)
    ya, s_t = _rwkv_chunk(r, lw, km, v, a, bb, g, s0, p, chunk)
    ob = attend(q, kb, zvb.reshape(b, t, DA_WIDTH))
    h, hn, top_e, top_g = _merge(ya.reshape(n, RW_WIDTH), ob.reshape(n, DA_WIDTH), zg, x2, p, tm)
    y = _moe(h, hn, top_e, top_g, moe_wts, tm)
    return (y.reshape(b, t, D_MODEL), k.reshape(b, t, DA_HEADS, 2 * DA_HEAD_DIM),
            zv.reshape(b, t, DA_HEADS, DA_V_DIM), s_t, zr3[:, -1])


def kernel(x_prompt, x_sample, cache_k, cache_v, page_table, state_wkv, state_shift, norm1_g, w_in, mu_shift, w0, w_w2, a0, a_a2, g_g2, k_k, k_a, r_k, ln_x_w, ln_x_b, q_norm_g, k_norm_g, lambda_q1, lambda_k1, lambda_q2, lambda_k2, subln_g, w_a, w_b, w_o, norm2_g, w_router, b_router, w_up, b_up, w_down, b_down):
    names = ["norm1_g", "mu_shift", "w0", "w_w2", "a0", "a_a2", "g_g2", "k_k", "k_a", "r_k",
             "ln_x_w", "ln_x_b", "q_norm_g", "k_norm_g", "lambda_q1", "lambda_k1", "lambda_q2",
             "lambda_k2", "subln_g", "w_a", "w_b", "w_o", "norm2_g", "w_router", "b_router"]
    vals = [norm1_g, mu_shift, w0, w_w2, a0, a_a2, g_g2, k_k, k_a, r_k, ln_x_w, ln_x_b,
            q_norm_g, k_norm_g, lambda_q1, lambda_k1, lambda_q2, lambda_k2, subln_g,
            w_a, w_b, w_o, norm2_g, w_router, b_router]
    p = {nm: vv[0] for nm, vv in zip(names, vals)}
    w_in_bf = w_in[0].astype(BF)
    moe_wts = (w_up[0].astype(BF), b_up[0], w_down[0].astype(BF), b_down[0])

    bp, tp, _ = x_prompt.shape
    bs, ts, _ = x_sample.shape
    past = page_table.shape[1] * PAGE_SIZE

    attend_p = lambda q, kb, vb: _prompt_attention(q, kb, vb, p, _pick(tp, 1024))
    yp, kp, vp, wp, sp = _layer(
        x_prompt, jnp.arange(tp), jnp.zeros((bp, RW_HEADS, RW_HEAD_DIM, RW_HEAD_DIM), F32),
        jnp.zeros((bp, RW_IN), F32), p, w_in_bf, moe_wts, attend_p, _pick(tp, 64))

    attend_s = lambda q, kb, vb: _sample_attention(
        q, kb, vb, cache_k, cache_v, page_table, p, _pick(page_table.shape[1], 8))
    ys, ks, vs, ws, ss = _layer(
        x_sample, past + jnp.arange(ts), state_wkv[0], state_shift[0], p, w_in_bf, moe_wts,
        attend_s, ts)

    return (yp, ys, kp[None], vp[None], wp[None], sp[None],
            ks[None], vs[None], ws[None], ss[None])
```

```python
import functools
import math

import jax
import jax.numpy as jnp
import numpy as np
from jax import lax
from jax.experimental import pallas as pl
from jax.experimental.pallas import tpu as pltpu

F32 = jnp.float32
BF = jnp.bfloat16

D_MODEL = 1024
PAGE_SIZE = 128
RW_HEADS = 8
RW_HEAD_DIM = 64
RW_WIDTH = RW_HEADS * RW_HEAD_DIM
DECAY_LORA = 64
AAA_LORA = 64
GATE_LORA = 128
RW_IN = 3 * RW_WIDTH + DECAY_LORA + AAA_LORA + GATE_LORA
GN_EPS = 64e-5
DA_HEADS = 4
DA_HEAD_DIM = 64
DA_V_DIM = 2 * DA_HEAD_DIM
DA_WIDTH = DA_HEADS * DA_V_DIM
ROPE_DIM = DA_HEAD_DIM // 4
ROPE_THETA = 500000.0
N_IN = RW_IN + 3 * DA_WIDTH + 2 * D_MODEL
N_EXPERTS = 32
TOP_K = 4
D_FF = D_MODEL
SWIGLU_ALPHA = 1.702
SWIGLU_LIMIT = 7.0
NORM_EPS = 1e-5
LAMBDA_INIT = 0.8 - 0.6 * math.exp(-0.3 * 0)

VMEM_LIMIT = 56 * 1024 * 1024
MOE_ROWS = 256
ROW_TILES = D_MODEL // 128
assert ROW_TILES == 8

_NT = (((1,), (1,)), ((), ()))
_TN = (((0,), (0,)), ((), ()))


def _cp(sem, vmem=VMEM_LIMIT):
    return pltpu.CompilerParams(dimension_semantics=sem, vmem_limit_bytes=vmem)


def _dot(a, b):
    return jnp.dot(a, b, preferred_element_type=F32)


def _dg(a, b, dims):
    return lax.dot_general(a, b, dims, preferred_element_type=F32)


def _split(x):
    hi = x.astype(BF)
    lo = (x - hi.astype(F32)).astype(BF)
    return hi, lo


def _dot_exact_rhs(a, b_bf16):
    ah, al = _split(a)
    return _dot(ah, b_bf16) + _dot(al, b_bf16)


def _sigmoid(x):
    return 1.0 / (1.0 + jnp.exp(-x))


_SEGS = (0, RW_IN, RW_IN + DA_WIDTH, RW_IN + 2 * DA_WIDTH, RW_IN + 3 * DA_WIDTH, N_IN)


def _inproj_kernel(x_ref, g_ref, w_ref, zr_ref, zq_ref, zk_ref, zv_ref, zvb_ref, zg_ref):
    x = x_ref[...]
    xn = x * lax.rsqrt(jnp.mean(x * x, axis=-1, keepdims=True) + NORM_EPS) * g_ref[...]
    xb = xn.astype(BF)
    zr_ref[...] = _dot(xb, w_ref[:, _SEGS[0]:_SEGS[1]])
    zq_ref[...] = _dot(xb, w_ref[:, _SEGS[1]:_SEGS[2]])
    zk_ref[...] = _dot(xb, w_ref[:, _SEGS[2]:_SEGS[3]])
    zv = _dot(xb, w_ref[:, _SEGS[3]:_SEGS[4]])
    zv_ref[...] = zv
    zvb_ref[...] = zv.astype(BF)
    zg_ref[...] = _dot(xb, w_ref[:, _SEGS[4]:_SEGS[5]])


def _inproj(x2, g, w_bf, tm):
    n = x2.shape[0]
    row = lambda w: pl.BlockSpec((tm, w), lambda i: (i, 0))
    return pl.pallas_call(
        _inproj_kernel,
        grid=(n // tm,),
        in_specs=[row(D_MODEL),
                  pl.BlockSpec((1, D_MODEL), lambda i: (0, 0)),
                  pl.BlockSpec((D_MODEL, N_IN), lambda i: (0, 0))],
        out_specs=[row(RW_IN), row(DA_WIDTH), row(DA_WIDTH), row(DA_WIDTH), row(DA_WIDTH),
                   row(2 * D_MODEL)],
        out_shape=[jax.ShapeDtypeStruct((n, RW_IN), F32),
                   jax.ShapeDtypeStruct((n, DA_WIDTH), F32),
                   jax.ShapeDtypeStruct((n, DA_WIDTH), F32),
                   jax.ShapeDtypeStruct((n, DA_WIDTH), F32),
                   jax.ShapeDtypeStruct((n, DA_WIDTH), BF),
                   jax.ShapeDtypeStruct((n, 2 * D_MODEL), F32)],
        compiler_params=_cp(("parallel",)),
        name="inproj",
    )(x2, g, w_bf)


def _group_ones(width, group):
    i = np.arange(width)
    return jnp.asarray((i[:, None] // group) == (i[None, :] // group), dtype=BF)


def _qk_prep_kernel(zq_ref, zk_ref, gq_ref, gk_ref, cos_ref, sa_ref, sb_ref, j_ref,
                    q_ref, k_ref, kb_ref):
    cos = jnp.concatenate([cos_ref[...]] * (DA_WIDTH // 128), axis=-1)
    sa = jnp.concatenate([sa_ref[...]] * (DA_WIDTH // 128), axis=-1)
    sb = jnp.concatenate([sb_ref[...]] * (DA_WIDTH // 128), axis=-1)

    def norm_rope(z, g):
        ms = _dot_exact_rhs(z * z, j_ref[...]) * (1.0 / DA_HEAD_DIM)
        y = z * lax.rsqrt(ms + NORM_EPS) * g
        half = ROPE_DIM // 2
        up = pltpu.roll(y, DA_WIDTH - half, axis=1)
        dn = pltpu.roll(y, half, axis=1)
        return y * cos + up * sa + dn * sb

    q = norm_rope(zq_ref[0], gq_ref[...])
    k = norm_rope(zk_ref[0], gk_ref[...])
    q_ref[0] = (q * (DA_HEAD_DIM ** -0.5)).astype(BF)
    k_ref[0] = k
    kb_ref[0] = k.astype(BF)


def _rope_tables(pos):
    half = ROPE_DIM // 2
    inv = ROPE_THETA ** (-jnp.arange(half, dtype=F32) / half)
    ang = pos.astype(F32)[:, None] * inv[None, :]
    lane = np.arange(128) % DA_HEAD_DIM
    idx = lane % half
    c = jnp.cos(ang)[:, idx]
    s = jnp.sin(ang)[:, idx]
    in_rope = jnp.asarray(lane < ROPE_DIM)
    first = jnp.asarray(lane < half)
    second = jnp.asarray((lane >= half) & (lane < ROPE_DIM))
    cos_t = jnp.where(in_rope[None, :], c, 1.0)
    sa_t = jnp.where(first[None, :], -s, 0.0)
    sb_t = jnp.where(second[None, :], s, 0.0)
    return cos_t, sa_t, sb_t


def _qk_prep(zq, zk, gq, gk, pos, tt):
    b, t, _ = zq.shape
    cos_t, sa_t, sb_t = _rope_tables(pos)
    gq_t = jnp.tile(gq, DA_WIDTH // DA_HEAD_DIM)[None, :]
    gk_t = jnp.tile(gk, DA_WIDTH // DA_HEAD_DIM)[None, :]
    tok = pl.BlockSpec((1, tt, DA_WIDTH), lambda i, j: (i, j, 0))
    tab = pl.BlockSpec((tt, 128), lambda i, j: (j, 0))
    par = pl.BlockSpec((1, DA_WIDTH), lambda i, j: (0, 0))
    return pl.pallas_call(
        _qk_prep_kernel,
        grid=(b, t // tt),
        in_specs=[tok, tok, par, par, tab, tab, tab,
                  pl.BlockSpec((DA_WIDTH, DA_WIDTH), lambda i, j: (0, 0))],
        out_specs=[tok, tok, tok],
        out_shape=[jax.ShapeDtypeStruct((b, t, DA_WIDTH), BF),
                   jax.ShapeDtypeStruct((b, t, DA_WIDTH), F32),
                   jax.ShapeDtypeStruct((b, t, DA_WIDTH), BF)],
        compiler_params=_cp(("parallel", "parallel")),
        name="qk_prep",
    )(zq, zk, gq_t, gk_t, cos_t, sa_t, sb_t, _group_ones(DA_WIDTH, DA_HEAD_DIM))


def _rwkv_prep_kernel(zr_ref, sh_ref, mu_ref, w0_ref, a0_ref, kk_ref, ka_ref,
                      w2_ref, a2_ref, g2_ref, j_ref,
                      r_ref, lw_ref, k_ref, v_ref, a_ref, b_ref, g_ref, carry_ref):
    tt = zr_ref.shape[1]

    @pl.when(pl.program_id(1) == 0)
    def _():
        carry_ref[...] = sh_ref[0]

    z = zr_ref[0]
    rolled = pltpu.roll(z, 1, axis=0)
    first_row = lax.broadcasted_iota(jnp.int32, (tt, 1), 0) == 0
    zprev = jnp.where(first_row, carry_ref[...], rolled)
    carry_ref[...] = z[tt - 1:tt, :]
    zs = z + (zprev - z) * mu_ref[...]

    r = zs[:, 0:RW_WIDTH]
    k = zs[:, RW_WIDTH:2 * RW_WIDTH]
    v = zs[:, 2 * RW_WIDTH:3 * RW_WIDTH]
    wa = zs[:, 3 * RW_WIDTH:3 * RW_WIDTH + DECAY_LORA + AAA_LORA]
    gd = zs[:, 3 * RW_WIDTH + DECAY_LORA + AAA_LORA:]

    lw = w0_ref[...] + _dot(jnp.tanh(wa).astype(BF), w2_ref[...])
    log_decay = -math.exp(-0.5) * _sigmoid(lw)
    a = _sigmoid(a0_ref[...] + _dot(wa.astype(BF), a2_ref[...]))
    g = _dot(_sigmoid(gd).astype(BF), g2_ref[...])
    kk = k * kk_ref[...]
    ss = _dot_exact_rhs(kk * kk, j_ref[...])
    kk = kk / jnp.maximum(jnp.sqrt(ss), 1e-12)
    kmod = k * (1.0 + (a - 1.0) * ka_ref[...])

    r_ref[0] = r
    lw_ref[0] = log_decay
    k_ref[0] = kmod
    v_ref[0] = v
    a_ref[0] = -kk
    b_ref[0] = kk * a
    g_ref[0] = g


def _rwkv_prep(zr, shift0, p, tt):
    b, t, _ = zr.shape
    zero_pad = jnp.zeros((DECAY_LORA, RW_WIDTH), F32)
    w2p = jnp.concatenate([p["w_w2"], zero_pad], axis=0).astype(BF)
    a2p = jnp.concatenate([zero_pad, p["a_a2"]], axis=0).astype(BF)
    tok = lambda w: pl.BlockSpec((1, tt, w), lambda i, j: (i, j, 0))
    par = lambda w: pl.BlockSpec((1, w), lambda i, j: (0, 0))
    mat = lambda r, c: pl.BlockSpec((r, c), lambda i, j: (0, 0))
    return pl.pallas_call(
        _rwkv_prep_kernel,
        grid=(b, t // tt),
        in_specs=[tok(RW_IN), pl.BlockSpec((1, 1, RW_IN), lambda i, j: (i, 0, 0)),
                  par(RW_IN), par(RW_WIDTH), par(RW_WIDTH), par(RW_WIDTH), par(RW_WIDTH),
                  mat(DECAY_LORA + AAA_LORA, RW_WIDTH), mat(DECAY_LORA + AAA_LORA, RW_WIDTH),
                  mat(GATE_LORA, RW_WIDTH), mat(RW_WIDTH, RW_WIDTH)],
        out_specs=[tok(RW_WIDTH)] * 7,
        out_shape=[jax.ShapeDtypeStruct((b, t, RW_WIDTH), F32)] * 7,
        scratch_shapes=[pltpu.VMEM((1, RW_IN), F32)],
        compiler_params=_cp(("parallel", "arbitrary")),
        name="rwkv_prep",
    )(zr, shift0[:, None, :], p["mu_shift"][None, :], p["w0"][None, :], p["a0"][None, :],
      p["k_k"][None, :], p["k_a"][None, :], w2p, a2p, p["g_g2"].astype(BF),
      _group_ones(RW_WIDTH, RW_HEAD_DIM))


_NN = (((1,), (0,)), ((), ()))


def _d3(a, b, dims=_NN):
    return _dg(a[0], b[0], dims) + _dg(a[0], b[1], dims) + _dg(a[1], b[0], dims)


def _rwkv_chunk_kernel(r_ref, lw_ref, k_ref, v_ref, a_ref, b_ref, g_ref, s0_ref,
                       lnw_ref, lnb_ref, rk_ref, y_ref, sT_ref, s_ref, *, c):
    n = RW_HEAD_DIM
    n_sub = r_ref.shape[1] // c
    heads = range(RW_HEADS)
    units = [(j, h) for j in range(n_sub) for h in heads]

    @pl.when(pl.program_id(1) == 0)
    def _():
        s_ref[...] = s0_ref[0]

    row = lax.broadcasted_iota(jnp.int32, (c, c), 0)
    col = lax.broadcasted_iota(jnp.int32, (c, c), 1)
    lower = row >= col
    strict = row > col
    eye_c = (row == col).astype(F32)
    tri = _split(lower.astype(F32))

    f32 = {}
    p_chunk = []
    for j in range(n_sub):
        rows = slice(j * c, (j + 1) * c)
        r, lw, k, v, a, b = (ref[0, rows, :] for ref in (r_ref, lw_ref, k_ref, v_ref, a_ref, b_ref))
        cum = _d3(tri, _split(lw))
        cum_last = cum[c - 1:c, :]
        pinv = jnp.exp(-cum)
        pc = jnp.exp(cum_last - cum)
        p_chunk.append(jnp.exp(cum_last))
        full = {"rt": r * jnp.exp(cum), "at": a * jnp.exp(cum - lw), "bt": b * pinv,
                "kt": k * pinv, "bh": b * pc, "kh": k * pc, "v": v, "rk": r * k * rk_ref[...]}
        for nm, x in full.items():
            f32.setdefault(nm, []).extend(x[:, h * n:(h + 1) * n] for h in heads)

    idx = range(len(units))
    bf = lambda xs: [x.astype(BF) for x in xs]
    at, bt, kt, bh, vv = (bf(f32[nm]) for nm in ("at", "bt", "kt", "bh", "v"))
    at_rt = bf([jnp.concatenate([f32["at"][u], f32["rt"][u]], axis=0) for u in idx])

    xb = [_dg(at_rt[u], bt[u], _NT) for u in idx]
    xk = [_dg(at_rt[u], kt[u], _NT) for u in idx]
    a_ab = [jnp.where(strict, x[:c], 0.0) for x in xb]
    a_ak = bf([jnp.where(strict, x[:c], 0.0) for x in xk])
    a_rb = bf([jnp.where(lower, x[c:], 0.0) for x in xb])
    a_rk = bf([jnp.where(lower, x[c:], 0.0) for x in xk])

    t_inv = [eye_c + x for x in a_ab]
    n_sq = int(math.log2(c)) - 1
    if n_sq > 0:
        pw = bf(a_ab)
        pw = [_dot(x, x) for x in pw]
        for it in range(n_sq):
            pw_b = bf(pw)
            if it + 1 < n_sq:
                both = [_dot(jnp.concatenate([t_inv[u].astype(BF), pw_b[u]], axis=0), pw_b[u])
                        for u in idx]
                t_inv = [t_inv[u] + both[u][:c] for u in idx]
                pw = [x[c:] for x in both]
            else:
                t_inv = [t_inv[u] + _dot(t_inv[u].astype(BF), pw_b[u]) for u in idx]
    t_b = bf(t_inv)
    akv = bf([_dot(a_ak[u], vv[u]) for u in idx])
    w1 = bf([_dot(t_b[u], at[u]) for u in idx])
    u0 = bf([_dot(t_b[u], akv[u]) for u in idx])

    krow = lax.broadcasted_iota(jnp.int32, (n, n), 0)
    kcol = lax.broadcasted_iota(jnp.int32, (n, n), 1)
    eye_n = krow == kcol
    m_mat = [_split(jnp.where(eye_n, p_chunk[j][:, h * n:(h + 1) * n], 0.0)
                    + _dg(w1[u], bh[u], _TN)) for u, (j, h) in enumerate(units)]
    n_mat = [_dg(u0[u], bh[u], _TN) + _d3(_split(f32["v"][u]), _split(f32["kh"][u]), _TN) for u in idx]
    g_mat = bf([f32["rt"][u] + _dot(a_rb[u], w1[u]) for u in idx])
    y0 = [_dot(a_rb[u], u0[u]) + _dot(a_rk[u], vv[u]) for u in idx]

    state = [s_ref[h] for h in heads]
    for j in range(n_sub):
        s_old = [_split(state[h]) for h in heads]
        ys = [_dg(g_mat[j * RW_HEADS + h], s_old[h][0], _NT) + y0[j * RW_HEADS + h] for h in heads]
        state = [_d3(s_old[h], m_mat[j * RW_HEADS + h]) + n_mat[j * RW_HEADS + h] for h in heads]
        yn, bonus = [], []
        for h in heads:
            u = j * RW_HEADS + h
            mu = jnp.mean(ys[h], axis=-1, keepdims=True)
            var = jnp.mean(jnp.square(ys[h] - mu), axis=-1, keepdims=True)
            yn.append((ys[h] - mu) * lax.rsqrt(var + GN_EPS))
            bonus.append(jnp.sum(f32["rk"][u], axis=-1, keepdims=True) * f32["v"][u])
        yn = jnp.concatenate(yn, axis=-1)
        bonus = jnp.concatenate(bonus, axis=-1)
        rows = slice(j * c, (j + 1) * c)
        y_ref[0, rows, :] = ((yn * lnw_ref[...] + lnb_ref[...] + bonus) * g_ref[0, rows, :]).astype(BF)
    for h in heads:
        s_ref[h] = state[h]

    @pl.when(pl.program_id(1) == pl.num_programs(1) - 1)
    def _():
        sT_ref[0] = s_ref[...]


def _rwkv_chunk(r, lw, k, v, a, b, g, s0, p, c, n_sub):
    bsz, t, _ = r.shape
    tok = pl.BlockSpec((1, c * n_sub, RW_WIDTH), lambda i, j: (i, j, 0))
    par = pl.BlockSpec((1, RW_WIDTH), lambda i, j: (0, 0))
    st = pl.BlockSpec((1, RW_HEADS, RW_HEAD_DIM, RW_HEAD_DIM), lambda i, j: (i, 0, 0, 0))
    return pl.pallas_call(
        functools.partial(_rwkv_chunk_kernel, c=c),
        grid=(bsz, t // (c * n_sub)),
        in_specs=[tok] * 7 + [st, par, par, par],
        out_specs=[tok, st],
        out_shape=[jax.ShapeDtypeStruct((bsz, t, RW_WIDTH), BF),
                   jax.ShapeDtypeStruct((bsz, RW_HEADS, RW_HEAD_DIM, RW_HEAD_DIM), F32)],
        scratch_shapes=[pltpu.VMEM((RW_HEADS, RW_HEAD_DIM, RW_HEAD_DIM), F32)],
        compiler_params=_cp(("parallel", "arbitrary")),
        name="rwkv_chunk",
    )(r, lw, k, v, a, b, g, s0, p["ln_x_w"][None, :], p["ln_x_b"][None, :],
      p["r_k"].reshape(1, RW_WIDTH))


def _lambda_full(lq1, lk1, lq2, lk2):
    s1 = jnp.sum(lq1 * lk1, axis=-1, keepdims=True)
    s2 = jnp.sum(lq2 * lk2, axis=-1, keepdims=True)
    return jnp.exp(s1) - jnp.exp(s2) + LAMBDA_INIT


def _subln(o, g):
    y = o * lax.rsqrt(jnp.mean(o * o, axis=-1, keepdims=True) + NORM_EPS)
    return y * g * (1.0 - LAMBDA_INIT)


def _prompt_attn_kernel(qi_ref, ki_ref, q_ref, k_ref, v_ref, lq1, lk1, lq2, lk2, sg_ref,
                        o_ref, qm_ref, m_ref, l_ref, acc_ref):
    p = pl.program_id(2)
    qi = qi_ref[p]
    ki = ki_ref[p]
    tq = q_ref.shape[1]
    tk = k_ref.shape[1]

    @pl.when(ki == 0)
    def _():
        q = q_ref[0]
        lane = lax.broadcasted_iota(jnp.int32, q.shape, 1)
        zero = jnp.zeros_like(q)
        qm_ref[0:tq] = jnp.where(lane < DA_HEAD_DIM, q, zero)
        qm_ref[tq:2 * tq] = jnp.where(lane >= DA_HEAD_DIM, q, zero)
        m_ref[...] = jnp.full(m_ref.shape, -jnp.inf, F32)
        l_ref[...] = jnp.zeros(l_ref.shape, F32)
        acc_ref[...] = jnp.zeros(acc_ref.shape, F32)

    def step(masked):
        s = _dg(qm_ref[...], k_ref[0], _NT)
        if masked:
            rowi = lax.broadcasted_iota(jnp.int32, (2 * tq, tk), 0)
            coli = lax.broadcasted_iota(jnp.int32, (2 * tq, tk), 1)
            s = jnp.where(coli <= jnp.where(rowi >= tq, rowi - tq, rowi), s, -jnp.inf)
        m_old = m_ref[...]
        m_new = jnp.maximum(m_old, jnp.max(s, axis=-1, keepdims=True))
        alpha = jnp.exp(m_old - m_new)
        pr = jnp.exp(s - jnp.concatenate([m_new] * (tk // 128), axis=-1))
        l_ref[...] = alpha * l_ref[...] + jnp.sum(pr, axis=-1, keepdims=True)
        acc_ref[...] = alpha * acc_ref[...] + _dot(pr.astype(BF), v_ref[0])
        m_ref[...] = m_new

    @pl.when(ki < qi)
    def _():
        step(False)

    @pl.when(ki == qi)
    def _():
        step(True)
        lam = _lambda_full(lq1[...], lk1[...], lq2[...], lk2[...])
        o = acc_ref[...] / l_ref[...]
        o_ref[0] = _subln(o[0:tq] - lam * o[tq:2 * tq], sg_ref[...]).astype(BF)


def _prompt_attention(q, kb, vb, p, tq):
    bsz, t, _ = q.shape
    nq = t // tq
    qi = np.concatenate([np.full(i + 1, i) for i in range(nq)]).astype(np.int32)
    ki = np.concatenate([np.arange(i + 1) for i in range(nq)]).astype(np.int32)
    lam_spec = pl.BlockSpec((1, DA_HEAD_DIM), lambda b, h, s, qt, kt: (0, 0))
    grid_spec = pltpu.PrefetchScalarGridSpec(
        num_scalar_prefetch=2,
        grid=(bsz, DA_HEADS, len(qi)),
        in_specs=[pl.BlockSpec((1, tq, DA_V_DIM), lambda b, h, s, qt, kt: (b, qt[s], h)),
                  pl.BlockSpec((1, tq, DA_V_DIM), lambda b, h, s, qt, kt: (b, kt[s], h)),
                  pl.BlockSpec((1, tq, DA_V_DIM), lambda b, h, s, qt, kt: (b, kt[s], h)),
                  lam_spec, lam_spec, lam_spec, lam_spec,
                  pl.BlockSpec((1, DA_V_DIM), lambda b, h, s, qt, kt: (0, 0))],
        out_specs=pl.BlockSpec((1, tq, DA_V_DIM), lambda b, h, s, qt, kt: (b, qt[s], h)),
        scratch_shapes=[pltpu.VMEM((2 * tq, DA_V_DIM), BF),
                        pltpu.VMEM((2 * tq, 128), F32),
                        pltpu.VMEM((2 * tq, 128), F32),
                        pltpu.VMEM((2 * tq, DA_V_DIM), F32)])
    return pl.pallas_call(
        _prompt_attn_kernel,
        grid_spec=grid_spec,
        out_shape=jax.ShapeDtypeStruct((bsz, t, DA_WIDTH), BF),
        compiler_params=_cp(("parallel", "parallel", "arbitrary")),
        name="prompt_attn",
    )(jnp.asarray(qi), jnp.asarray(ki), q, kb, vb,
      p["lambda_q1"][None, :], p["lambda_k1"][None, :], p["lambda_q2"][None, :],
      p["lambda_k2"][None, :], p["subln_g"][None, :])


def _make_sample_attn_kernel(pages):
    n_pairs = DA_HEADS * 2

    def kernel(pt_ref, q_ref, kn_ref, vn_ref, lq1, lk1, lq2, lk2, sg_ref, *rest):
        k_refs = rest[:pages]
        v_refs = rest[pages:2 * pages]
        o_ref = rest[2 * pages]
        qall_ref, m_ref, l_ref, acc_ref = rest[2 * pages + 1:]
        j = pl.program_id(1)
        s_len = q_ref.shape[1]
        rows = n_pairs * s_len
        page_rows = PAGE_SIZE * DA_HEADS

        @pl.when(j == 0)
        def _():
            q = q_ref[0].astype(F32)
            lane = lax.broadcasted_iota(jnp.int32, (s_len, DA_V_DIM), 1)
            pieces = []
            for h in range(DA_HEADS):
                qh = q[:, h * DA_V_DIM:(h + 1) * DA_V_DIM]
                pieces.append(jnp.where(lane < DA_HEAD_DIM, qh, 0.0))
                pieces.append(jnp.where(lane >= DA_HEAD_DIM, qh, 0.0))
            qall_ref[...] = jnp.concatenate(pieces, axis=0).astype(BF)
            m_ref[...] = jnp.full(m_ref.shape, -jnp.inf, F32)
            l_ref[...] = jnp.zeros(l_ref.shape, F32)
            acc_ref[...] = jnp.zeros(acc_ref.shape, F32)

        def update(s, vals):
            m_old = m_ref[...]
            m_new = jnp.maximum(m_old, jnp.max(s, axis=-1, keepdims=True))
            alpha = jnp.exp(m_old - m_new)
            pr = jnp.exp(s - m_new)
            l_ref[...] = alpha * l_ref[...] + jnp.sum(pr, axis=-1, keepdims=True)
            acc_ref[...] = alpha * acc_ref[...] + _dot(pr.astype(BF), vals)
            m_ref[...] = m_new

        qall = qall_ref[...]
        flat = lambda ref: ref[...].reshape(page_rows, DA_V_DIM).astype(BF)
        s_all = jnp.concatenate([_dg(qall, flat(kr), _NT) for kr in k_refs], axis=-1)
        v_all = jnp.concatenate([flat(vr) for vr in v_refs], axis=0)
        row_head = lax.broadcasted_iota(jnp.int32, s_all.shape, 0) // (2 * s_len)
        key_head = lax.broadcasted_iota(jnp.int32, s_all.shape, 1) % DA_HEADS
        update(jnp.where(row_head == key_head, s_all, -jnp.inf), v_all)

        @pl.when(j == pl.num_programs(1) - 1)
        def _():
            kn = kn_ref[0].astype(F32)
            vn = vn_ref[0].astype(F32)
            by_head = lambda x: jnp.concatenate(
                [x[:, h * DA_V_DIM:(h + 1) * DA_V_DIM] for h in range(DA_HEADS)], axis=0).astype(BF)
            s = _dg(qall, by_head(kn), _NT)
            r = lax.broadcasted_iota(jnp.int32, s.shape, 0)
            c = lax.broadcasted_iota(jnp.int32, s.shape, 1)
            keep = (c // s_len == r // (2 * s_len)) & (c % s_len <= r % s_len)
            update(jnp.where(keep, s, -jnp.inf), by_head(vn))
            lam = _lambda_full(lq1[...], lk1[...], lq2[...], lk2[...])
            o = acc_ref[...] / l_ref[...]
            heads = []
            for h in range(DA_HEADS):
                o1 = o[(2 * h) * s_len:(2 * h + 1) * s_len]
                o2 = o[(2 * h + 1) * s_len:(2 * h + 2) * s_len]
                heads.append(_subln(o1 - lam * o2, sg_ref[...]))
            o_ref[0] = jnp.concatenate(heads, axis=-1).astype(BF)

    return kernel


def _sample_attention(q, kb, vb, cache_k, cache_v, page_table, p, pages):
    bsz, s_len, _ = q.shape
    n_pages = page_table.shape[1]
    rows = DA_HEADS * 2 * s_len
    tok = pl.BlockSpec((1, s_len, DA_WIDTH), lambda b, j, pt: (b, 0, 0))
    lam_spec = pl.BlockSpec((1, DA_HEAD_DIM), lambda b, j, pt: (0, 0))

    def page_spec(i):
        return pl.BlockSpec((None, None, PAGE_SIZE, DA_HEADS, DA_V_DIM),
                            lambda b, j, pt: (0, pt[b * n_pages + j * pages + i], 0, 0, 0))

    grid_spec = pltpu.PrefetchScalarGridSpec(
        num_scalar_prefetch=1,
        grid=(bsz, n_pages // pages),
        in_specs=[tok, tok, tok, lam_spec, lam_spec, lam_spec, lam_spec,
                  pl.BlockSpec((1, DA_V_DIM), lambda b, j, pt: (0, 0))]
                 + [page_spec(i) for i in range(pages)] * 2,
        out_specs=tok,
        scratch_shapes=[pltpu.VMEM((rows, DA_V_DIM), BF),
                        pltpu.VMEM((rows, 1), F32),
                        pltpu.VMEM((rows, 1), F32),
                        pltpu.VMEM((rows, DA_V_DIM), F32)])
    return pl.pallas_call(
        _make_sample_attn_kernel(pages),
        grid_spec=grid_spec,
        out_shape=jax.ShapeDtypeStruct((bsz, s_len, DA_WIDTH), BF),
        compiler_params=_cp(("parallel", "arbitrary")),
        name="sample_attn",
    )(page_table.reshape(-1), q, kb, vb,
      p["lambda_q1"][None, :], p["lambda_k1"][None, :], p["lambda_q2"][None, :],
      p["lambda_k2"][None, :], p["subln_g"][None, :], *([cache_k] * pages), *([cache_v] * pages))


def _merge_kernel(ya_ref, ob_ref, zg_ref, x_ref, wa_ref, wb_ref, wo_ref, g2_ref,
                  wr_hi_ref, wr_lo_ref, br_ref, h_ref, hn_ref, te_ref, tg_ref):
    zg = zg_ref[...]
    m = (_sigmoid(zg[:, :D_MODEL]) * _dot(ya_ref[...], wa_ref[...])
         + _sigmoid(zg[:, D_MODEL:]) * _dot(ob_ref[...], wb_ref[...]))
    h = x_ref[...] + _dot(m.astype(BF), wo_ref[...])
    h_ref[...] = h
    hn = h * lax.rsqrt(jnp.mean(h * h, axis=-1, keepdims=True) + NORM_EPS) * g2_ref[...]
    tm = hn.shape[0]
    for j in range(ROW_TILES):
        hn_ref[pl.ds(j, tm, stride=ROW_TILES), :] = hn[:, j * 128:(j + 1) * 128]

    hh, hl = _split(hn)
    logits = (_dg(wr_hi_ref[...], hh, _NT) + _dg(wr_hi_ref[...], hl, _NT)
              + _dg(wr_lo_ref[...], hh, _NT)) + br_ref[...]
    eidx = lax.broadcasted_iota(jnp.int32, logits.shape, 0)
    vals, idxs = [], []
    for _ in range(TOP_K):
        mx = jnp.max(logits, axis=0, keepdims=True)
        am = jnp.min(jnp.where(logits == mx, eidx, N_EXPERTS), axis=0, keepdims=True)
        vals.append(mx)
        idxs.append(am)
        logits = jnp.where(eidx == am, -jnp.inf, logits)
    ex = [jnp.exp(vv - vals[0]) for vv in vals]
    den = ex[0] + ex[1] + ex[2] + ex[3]
    te_ref[...] = jnp.concatenate(idxs, axis=0)
    tg_ref[...] = jnp.concatenate([e / den for e in ex], axis=0)


def _merge(ya, ob, zg, x2, p, tm):
    n = x2.shape[0]
    wr_t = p["w_router"].T
    wr_hi = wr_t.astype(BF)
    wr_lo = (wr_t - wr_hi.astype(F32)).astype(BF)
    row = lambda w: pl.BlockSpec((tm, w), lambda i: (i, 0))
    mat = lambda r, c: pl.BlockSpec((r, c), lambda i: (0, 0))
    colblk = pl.BlockSpec((TOP_K, tm), lambda i: (0, i))
    return pl.pallas_call(
        _merge_kernel,
        grid=(n // tm,),
        in_specs=[row(RW_WIDTH), row(DA_WIDTH), row(2 * D_MODEL), row(D_MODEL),
                  mat(RW_WIDTH, D_MODEL), mat(DA_WIDTH, D_MODEL), mat(D_MODEL, D_MODEL),
                  mat(1, D_MODEL), mat(N_EXPERTS, D_MODEL), mat(N_EXPERTS, D_MODEL),
                  mat(N_EXPERTS, 1)],
        out_specs=[row(D_MODEL), pl.BlockSpec((tm * ROW_TILES, 128), lambda i: (i, 0)),
                   colblk, colblk],
        out_shape=[jax.ShapeDtypeStruct((n, D_MODEL), F32),
                   jax.ShapeDtypeStruct((n * ROW_TILES, 128), F32),
                   jax.ShapeDtypeStruct((TOP_K, n), jnp.int32),
                   jax.ShapeDtypeStruct((TOP_K, n), F32)],
        compiler_params=_cp(("parallel",)),
        name="merge_router",
    )(ya, ob, zg, x2, p["w_a"].astype(BF), p["w_b"].astype(BF), p["w_o"].astype(BF),
      p["norm2_g"][None, :], wr_hi, wr_lo, p["b_router"][:, None])


def _experts_kernel(ib_ref, ie_ref, ilo_ref, ihi_ref, ifirst_ref, ilast_ref,
                    idx_hbm, x_hbm, wu_ref, bu_ref, wd_ref, bd_ref, out_hbm,
                    idx_smem, idx_sem, xbuf, gsem, xb_ref, acc_ref, obuf, ssem):
    i = pl.program_id(0)
    bm = MOE_ROWS
    slab = bm * ROW_TILES
    n_blocks = idx_hbm.shape[0]
    blk = ib_ref[i]
    s2 = blk % 2
    s3 = blk % 3

    def idx_copy(b, slot3):
        return pltpu.make_async_copy(idx_hbm.at[b], idx_smem.at[pl.ds(slot3 * 2 * bm, 2 * bm)],
                                     idx_sem.at[slot3])

    def tile_rows(row):
        return pl.ds(pl.multiple_of(row * ROW_TILES, ROW_TILES), ROW_TILES)

    def gather_rows(slot3, slot2):
        def body(r, carry):
            pltpu.make_async_copy(x_hbm.at[tile_rows(idx_smem[slot3 * 2 * bm + r])],
                                  xbuf.at[tile_rows(slot2 * bm + r)], gsem.at[slot2]).start()
            return carry
        lax.fori_loop(0, bm, body, 0, unroll=8)

    def scatter_rows(slot3, slot2):
        def body(r, carry):
            pltpu.make_async_copy(obuf.at[tile_rows(slot2 * bm + r)],
                                  out_hbm.at[tile_rows(idx_smem[slot3 * 2 * bm + bm + r])], ssem.at[slot2]).start()
            return carry
        lax.fori_loop(0, bm, body, 0, unroll=8)

    def block_rows(slot2):
        return pl.ds(pl.multiple_of(slot2 * slab, slab), slab)

    def wait_gather(slot2):
        pltpu.make_async_copy(x_hbm.at[pl.ds(0, slab)], xbuf.at[block_rows(slot2)], gsem.at[slot2]).wait()

    def wait_scatter(slot2):
        pltpu.make_async_copy(obuf.at[block_rows(slot2)], out_hbm.at[pl.ds(0, slab)], ssem.at[slot2]).wait()

    @pl.when(i == 0)
    def _():
        idx_copy(0, 0).start()
        idx_copy(0, 0).wait()
        gather_rows(0, 0)
        if n_blocks > 1:
            idx_copy(1, 1).start()

    @pl.when(ifirst_ref[i] == 1)
    def _():
        wait_gather(s2)

        @pl.when(blk + 1 < n_blocks)
        def _():
            nxt = (blk + 1) % 3
            idx_copy(blk + 1, nxt).wait()
            gather_rows(nxt, 1 - s2)

        @pl.when(blk + 2 < n_blocks)
        def _():
            idx_copy(blk + 2, (blk + 2) % 3).start()

        base = s2 * slab
        x = jnp.concatenate([xbuf[pl.ds(base + j, bm, stride=ROW_TILES), :] for j in range(ROW_TILES)],
                            axis=-1)
        xb_ref[...] = x.astype(BF)
        acc_ref[...] = jnp.zeros(acc_ref.shape, F32)

    lo = ilo_ref[i]
    hi = ihi_ref[i]

    @pl.when(hi > lo)
    def _():
        h = _dot(xb_ref[...], wu_ref[...]) + bu_ref[...]
        hg = jnp.minimum(h[:, :D_FF], SWIGLU_LIMIT)
        hl = jnp.clip(h[:, D_FF:], -SWIGLU_LIMIT, SWIGLU_LIMIT)
        act = hg * _sigmoid(SWIGLU_ALPHA * hg) * (hl + 1.0)
        y = _dot(act.astype(BF), wd_ref[...]) + bd_ref[...]
        rowi = lax.broadcasted_iota(jnp.int32, (bm, 1), 0)
        acc_ref[...] += jnp.where((rowi >= lo) & (rowi < hi), y, 0.0)

    @pl.when(ilast_ref[i] == 1)
    def _():
        @pl.when(blk >= 2)
        def _():
            wait_scatter(s2)

        base = s2 * slab
        for j in range(ROW_TILES):
            obuf[pl.ds(base + j, bm, stride=ROW_TILES), :] = acc_ref[:, j * 128:(j + 1) * 128]
        scatter_rows(s3, s2)

    @pl.when(i == pl.num_programs(0) - 1)
    def _():
        wait_scatter((n_blocks - 1) % 2)
        if n_blocks > 1:
            wait_scatter((n_blocks - 2) % 2)


def _experts(hn_rows, items, idx, w_up, b_up, w_down, b_down):
    bm = MOE_ROWS
    n_blocks = idx.shape[0]
    n_items = items[0].shape[0]
    wspec = lambda r, c: pl.BlockSpec((None, r, c), lambda i, ib, ie, *_: (ie[i], 0, 0))
    grid_spec = pltpu.PrefetchScalarGridSpec(
        num_scalar_prefetch=6,
        grid=(n_items,),
        in_specs=[pl.BlockSpec(memory_space=pl.ANY), pl.BlockSpec(memory_space=pl.ANY),
                  wspec(D_MODEL, 2 * D_FF), wspec(1, 2 * D_FF), wspec(D_FF, D_MODEL), wspec(1, D_MODEL)],
        out_specs=pl.BlockSpec(memory_space=pl.ANY),
        scratch_shapes=[pltpu.SMEM((3 * 2 * bm,), jnp.int32),
                        pltpu.SemaphoreType.DMA((3,)),
                        pltpu.VMEM((2 * bm * ROW_TILES, 128), F32),
                        pltpu.SemaphoreType.DMA((2,)),
                        pltpu.VMEM((bm, D_MODEL), BF),
                        pltpu.VMEM((bm, D_MODEL), F32),
                        pltpu.VMEM((2 * bm * ROW_TILES, 128), F32),
                        pltpu.SemaphoreType.DMA((2,))])
    return pl.pallas_call(
        _experts_kernel,
        grid_spec=grid_spec,
        out_shape=jax.ShapeDtypeStruct((n_blocks * bm * ROW_TILES, 128), F32),
        compiler_params=_cp(("arbitrary",)),
        name="experts",
    )(*items, idx, hn_rows, w_up, b_up[:, None, :], w_down, b_down[:, None, :])


def _combine_kernel(h_ref, g_ref, *rest):
    e_refs = rest[:TOP_K]
    o_ref = rest[TOP_K]
    tm = h_ref.shape[0]
    g = g_ref[...]
    for j in range(ROW_TILES):
        acc = h_ref[:, j * 128:(j + 1) * 128]
        for kk in range(TOP_K):
            acc = acc + g[:, kk:kk + 1] * e_refs[kk][pl.ds(j, tm, stride=ROW_TILES), :]
        o_ref[:, j * 128:(j + 1) * 128] = acc


def _combine(h, gates_t, expert_rows, tm):
    n = h.shape[0]
    nt = n // tm
    slot = lambda kk: pl.BlockSpec((tm * ROW_TILES, 128), lambda i: (kk * nt + i, 0))
    return pl.pallas_call(
        _combine_kernel,
        grid=(nt,),
        in_specs=[pl.BlockSpec((tm, D_MODEL), lambda i: (i, 0)),
                  pl.BlockSpec((tm, TOP_K), lambda i: (i, 0))]
                 + [slot(kk) for kk in range(TOP_K)],
        out_specs=pl.BlockSpec((tm, D_MODEL), lambda i: (i, 0)),
        out_shape=jax.ShapeDtypeStruct((n, D_MODEL), F32),
        compiler_params=_cp(("parallel",)),
        name="combine",
    )(h, gates_t, *([expert_rows] * TOP_K))


def _route(top_e):
    n = top_e.shape[1]
    nk = n * TOP_K
    bm = MOE_ROWS
    n_blocks = nk // bm
    n_items = n_blocks + N_EXPERTS
    flat_e = top_e.reshape(-1)
    _, sorted_slot = lax.sort((flat_e, jnp.arange(nk, dtype=jnp.int32)), num_keys=1)
    experts = jnp.arange(N_EXPERTS, dtype=jnp.int32)
    counts = jnp.sum((flat_e[None, :] == experts[:, None]).astype(jnp.int32), axis=1)
    end = jnp.cumsum(counts)
    start = end - counts
    first_blk = start // bm
    n_e = jnp.where(counts > 0, (end - 1) // bm - first_blk + 1, 0)
    cum = jnp.cumsum(n_e)
    off = cum - n_e
    total = cum[-1]
    it = jnp.arange(n_items, dtype=jnp.int32)
    valid = it < total
    e_of = lambda t: jnp.minimum(jnp.sum((cum[None, :] <= t[:, None]).astype(jnp.int32), axis=1),
                                 N_EXPERTS - 1)
    e_i = jnp.where(valid, e_of(it), e_of(total[None] - 1)[0])
    blk = jnp.where(valid, first_blk[e_i] + it - off[e_i], n_blocks - 1)
    lo = jnp.where(valid, jnp.clip(start[e_i] - blk * bm, 0, bm), 0)
    hi = jnp.where(valid, jnp.clip(end[e_i] - blk * bm, 0, bm), 0)
    prev_blk = jnp.concatenate([jnp.full((1,), -1, jnp.int32), blk[:-1]])
    next_blk = jnp.concatenate([blk[1:], jnp.full((1,), -1, jnp.int32)])
    first = valid & (blk != prev_blk)
    last = valid & ((blk != next_blk) | (it == total - 1))
    items = tuple(x.astype(jnp.int32) for x in (blk, e_i, lo, hi, first, last))
    idx = jnp.concatenate([(sorted_slot % n).reshape(n_blocks, bm), sorted_slot.reshape(n_blocks, bm)],
                          axis=1)
    return items, idx


def _moe(h, hn_rows, top_e, top_g, wts, tm):
    items, idx = _route(top_e)
    expert_rows = _experts(hn_rows, items, idx, *wts)
    return _combine(h, top_g.T, expert_rows, tm)


def _pick(n, pref):
    t = min(n, pref)
    assert n % t == 0
    return t


def _layer(x, pos, s0, shift0, p, w_in_bf, moe_wts, attend, chunk):
    b, t, _ = x.shape
    n = b * t
    tm = _pick(n, 256)
    tt = _pick(t, 256)
    x2 = x.reshape(n, D_MODEL)
    zr, zq, zk, zv, zvb, zg = _inproj(x2, p["norm1_g"][None, :], w_in_bf, tm)
    zr3 = zr.reshape(b, t, RW_IN)
    q, k, kb = _qk_prep(zq.reshape(b, t, DA_WIDTH), zk.reshape(b, t, DA_WIDTH),
                        p["q_norm_g"], p["k_norm_g"], pos, tt)
    r, lw, km, v, a, bb, g = _rwkv_prep(zr3, shift0, p, tt)
    ya, s_t = _rwkv_chunk(r, lw, km, v, a, bb, g, s0, p, chunk, _pick(t // chunk, 4))
    ob = attend(q, kb, zvb.reshape(b, t, DA_WIDTH))
    h, hn, top_e, top_g = _merge(ya.reshape(n, RW_WIDTH), ob.reshape(n, DA_WIDTH), zg, x2, p, tm)
    y = _moe(h, hn, top_e, top_g, moe_wts, tm)
    return (y.reshape(b, t, D_MODEL), k.reshape(b, t, DA_HEADS, 2 * DA_HEAD_DIM),
            zv.reshape(b, t, DA_HEADS, DA_V_DIM), s_t, zr3[:, -1])


def kernel(x_prompt, x_sample, cache_k, cache_v, page_table, state_wkv, state_shift, norm1_g, w_in, mu_shift, w0, w_w2, a0, a_a2, g_g2, k_k, k_a, r_k, ln_x_w, ln_x_b, q_norm_g, k_norm_g, lambda_q1, lambda_k1, lambda_q2, lambda_k2, subln_g, w_a, w_b, w_o, norm2_g, w_router, b_router, w_up, b_up, w_down, b_down):
    names = ["norm1_g", "mu_shift", "w0", "w_w2", "a0", "a_a2", "g_g2", "k_k", "k_a", "r_k",
             "ln_x_w", "ln_x_b", "q_norm_g", "k_norm_g", "lambda_q1", "lambda_k1", "lambda_q2",
             "lambda_k2", "subln_g", "w_a", "w_b", "w_o", "norm2_g", "w_router", "b_router"]
    vals = [norm1_g, mu_shift, w0, w_w2, a0, a_a2, g_g2, k_k, k_a, r_k, ln_x_w, ln_x_b,
            q_norm_g, k_norm_g, lambda_q1, lambda_k1, lambda_q2, lambda_k2, subln_g,
            w_a, w_b, w_o, norm2_g, w_router, b_router]
    p = {nm: vv[0] for nm, vv in zip(names, vals)}
    w_in_bf = w_in[0].astype(BF)
    moe_wts = (w_up[0].astype(BF), b_up[0], w_down[0].astype(BF), b_down[0])

    bp, tp, _ = x_prompt.shape
    bs, ts, _ = x_sample.shape
    past = page_table.shape[1] * PAGE_SIZE

    attend_p = lambda q, kb, vb: _prompt_attention(q, kb, vb, p, _pick(tp, 1024))
    yp, kp, vp, wp, sp = _layer(
        x_prompt, jnp.arange(tp), jnp.zeros((bp, RW_HEADS, RW_HEAD_DIM, RW_HEAD_DIM), F32),
        jnp.zeros((bp, RW_IN), F32), p, w_in_bf, moe_wts, attend_p, _pick(tp, 64))

    attend_s = lambda q, kb, vb: _sample_attention(
        q, kb, vb, cache_k, cache_v, page_table, p, _pick(page_table.shape[1], 8))
    ys, ks, vs, ws, ss = _layer(
        x_sample, past + jnp.arange(ts), state_wkv[0], state_shift[0], p, w_in_bf, moe_wts,
        attend_s, ts)

    return (yp, ys, kp[None], vp[None], wp[None], sp[None],
            ks[None], vs[None], ws[None], ss[None])
```

```python
import functools
import math

import jax
import jax.numpy as jnp
import numpy as np
from jax import lax
from jax.experimental import pallas as pl
from jax.experimental.pallas import tpu as pltpu

F32 = jnp.float32
BF = jnp.bfloat16

D_MODEL = 1024
PAGE_SIZE = 128
RW_HEADS = 8
RW_HEAD_DIM = 64
RW_WIDTH = RW_HEADS * RW_HEAD_DIM
DECAY_LORA = 64
AAA_LORA = 64
GATE_LORA = 128
RW_IN = 3 * RW_WIDTH + DECAY_LORA + AAA_LORA + GATE_LORA
GN_EPS = 64e-5
DA_HEADS = 4
DA_HEAD_DIM = 64
DA_V_DIM = 2 * DA_HEAD_DIM
DA_WIDTH = DA_HEADS * DA_V_DIM
ROPE_DIM = DA_HEAD_DIM // 4
ROPE_THETA = 500000.0
N_IN = RW_IN + 3 * DA_WIDTH + 2 * D_MODEL
N_EXPERTS = 32
TOP_K = 4
D_FF = D_MODEL
SWIGLU_ALPHA = 1.702
SWIGLU_LIMIT = 7.0
NORM_EPS = 1e-5
LOG2_E = 1.4426950408889634
LAMBDA_INIT = 0.8 - 0.6 * math.exp(-0.3 * 0)

VMEM_LIMIT = 56 * 1024 * 1024
MOE_ROWS = 256
IDX_SLOTS = 4
ROW_TILES = D_MODEL // 128
assert ROW_TILES == 8

_NT = (((1,), (1,)), ((), ()))
_TN = (((0,), (0,)), ((), ()))


def _cp(sem, vmem=VMEM_LIMIT):
    return pltpu.CompilerParams(dimension_semantics=sem, vmem_limit_bytes=vmem)


def _dot(a, b):
    return jnp.dot(a, b, preferred_element_type=F32)


def _dg(a, b, dims):
    return lax.dot_general(a, b, dims, preferred_element_type=F32)


def _split(x):
    hi = x.astype(BF)
    lo = (x - hi.astype(F32)).astype(BF)
    return hi, lo


def _dot_exact_rhs(a, b_bf16):
    ah, al = _split(a)
    return _dot(ah, b_bf16) + _dot(al, b_bf16)


def _sigmoid(x):
    return 1.0 / (1.0 + jnp.exp(-x))


_SEGS = (0, RW_IN, RW_IN + DA_WIDTH, RW_IN + 2 * DA_WIDTH, RW_IN + 3 * DA_WIDTH, N_IN)


def _inproj_kernel(x_ref, g_ref, w_ref, zr_ref, zq_ref, zk_ref, zv_ref, zvb_ref, zg_ref):
    x = x_ref[...]
    xn = x * lax.rsqrt(jnp.mean(x * x, axis=-1, keepdims=True) + NORM_EPS) * g_ref[...]
    xb = xn.astype(BF)
    zr_ref[...] = _dot(xb, w_ref[:, _SEGS[0]:_SEGS[1]])
    zq_ref[...] = _dot(xb, w_ref[:, _SEGS[1]:_SEGS[2]])
    zk_ref[...] = _dot(xb, w_ref[:, _SEGS[2]:_SEGS[3]])
    zv = _dot(xb, w_ref[:, _SEGS[3]:_SEGS[4]])
    zv_ref[...] = zv
    zvb_ref[...] = zv.astype(BF)
    zg_ref[...] = _dot(xb, w_ref[:, _SEGS[4]:_SEGS[5]])


def _inproj(x2, g, w_bf, tm):
    n = x2.shape[0]
    row = lambda w: pl.BlockSpec((tm, w), lambda i: (i, 0))
    return pl.pallas_call(
        _inproj_kernel,
        grid=(n // tm,),
        in_specs=[row(D_MODEL),
                  pl.BlockSpec((1, D_MODEL), lambda i: (0, 0)),
                  pl.BlockSpec((D_MODEL, N_IN), lambda i: (0, 0))],
        out_specs=[row(RW_IN), row(DA_WIDTH), row(DA_WIDTH), row(DA_WIDTH), row(DA_WIDTH),
                   row(2 * D_MODEL)],
        out_shape=[jax.ShapeDtypeStruct((n, RW_IN), F32),
                   jax.ShapeDtypeStruct((n, DA_WIDTH), F32),
                   jax.ShapeDtypeStruct((n, DA_WIDTH), F32),
                   jax.ShapeDtypeStruct((n, DA_WIDTH), F32),
                   jax.ShapeDtypeStruct((n, DA_WIDTH), BF),
                   jax.ShapeDtypeStruct((n, 2 * D_MODEL), F32)],
        compiler_params=_cp(("parallel",)),
        name="inproj",
    )(x2, g, w_bf)


def _group_ones(width, group):
    i = np.arange(width)
    return jnp.asarray((i[:, None] // group) == (i[None, :] // group), dtype=BF)


def _qk_prep_kernel(zq_ref, zk_ref, gq_ref, gk_ref, cos_ref, sa_ref, sb_ref, j_ref,
                    q_ref, k_ref, kb_ref):
    cos = jnp.concatenate([cos_ref[...]] * (DA_WIDTH // 128), axis=-1)
    sa = jnp.concatenate([sa_ref[...]] * (DA_WIDTH // 128), axis=-1)
    sb = jnp.concatenate([sb_ref[...]] * (DA_WIDTH // 128), axis=-1)

    def norm_rope(z, g):
        ms = _dot_exact_rhs(z * z, j_ref[...]) * (1.0 / DA_HEAD_DIM)
        y = z * lax.rsqrt(ms + NORM_EPS) * g
        half = ROPE_DIM // 2
        up = pltpu.roll(y, DA_WIDTH - half, axis=1)
        dn = pltpu.roll(y, half, axis=1)
        return y * cos + up * sa + dn * sb

    q = norm_rope(zq_ref[0], gq_ref[...])
    k = norm_rope(zk_ref[0], gk_ref[...])
    q_ref[0] = (q * (DA_HEAD_DIM ** -0.5 * LOG2_E)).astype(BF)
    k_ref[0] = k
    kb_ref[0] = k.astype(BF)


def _rope_tables(pos):
    half = ROPE_DIM // 2
    inv = ROPE_THETA ** (-jnp.arange(half, dtype=F32) / half)
    ang = pos.astype(F32)[:, None] * inv[None, :]
    lane = np.arange(128) % DA_HEAD_DIM
    idx = lane % half
    c = jnp.cos(ang)[:, idx]
    s = jnp.sin(ang)[:, idx]
    in_rope = jnp.asarray(lane < ROPE_DIM)
    first = jnp.asarray(lane < half)
    second = jnp.asarray((lane >= half) & (lane < ROPE_DIM))
    cos_t = jnp.where(in_rope[None, :], c, 1.0)
    sa_t = jnp.where(first[None, :], -s, 0.0)
    sb_t = jnp.where(second[None, :], s, 0.0)
    return cos_t, sa_t, sb_t


def _qk_prep(zq, zk, gq, gk, pos, tt):
    b, t, _ = zq.shape
    cos_t, sa_t, sb_t = _rope_tables(pos)
    gq_t = jnp.tile(gq, DA_WIDTH // DA_HEAD_DIM)[None, :]
    gk_t = jnp.tile(gk, DA_WIDTH // DA_HEAD_DIM)[None, :]
    tok = pl.BlockSpec((1, tt, DA_WIDTH), lambda i, j: (i, j, 0))
    tab = pl.BlockSpec((tt, 128), lambda i, j: (j, 0))
    par = pl.BlockSpec((1, DA_WIDTH), lambda i, j: (0, 0))
    return pl.pallas_call(
        _qk_prep_kernel,
        grid=(b, t // tt),
        in_specs=[tok, tok, par, par, tab, tab, tab,
                  pl.BlockSpec((DA_WIDTH, DA_WIDTH), lambda i, j: (0, 0))],
        out_specs=[tok, tok, tok],
        out_shape=[jax.ShapeDtypeStruct((b, t, DA_WIDTH), BF),
                   jax.ShapeDtypeStruct((b, t, DA_WIDTH), F32),
                   jax.ShapeDtypeStruct((b, t, DA_WIDTH), BF)],
        compiler_params=_cp(("parallel", "parallel")),
        name="qk_prep",
    )(zq, zk, gq_t, gk_t, cos_t, sa_t, sb_t, _group_ones(DA_WIDTH, DA_HEAD_DIM))


def _rwkv_prep_kernel(zr_ref, sh_ref, mu_ref, w0_ref, a0_ref, kk_ref, ka_ref,
                      w2_ref, a2_ref, g2_ref, j_ref,
                      r_ref, lw_ref, k_ref, v_ref, a_ref, b_ref, g_ref, carry_ref):
    tt = zr_ref.shape[1]

    @pl.when(pl.program_id(1) == 0)
    def _():
        carry_ref[...] = sh_ref[0]

    z = zr_ref[0]
    rolled = pltpu.roll(z, 1, axis=0)
    first_row = lax.broadcasted_iota(jnp.int32, (tt, 1), 0) == 0
    zprev = jnp.where(first_row, carry_ref[...], rolled)
    carry_ref[...] = z[tt - 1:tt, :]
    zs = z + (zprev - z) * mu_ref[...]

    r = zs[:, 0:RW_WIDTH]
    k = zs[:, RW_WIDTH:2 * RW_WIDTH]
    v = zs[:, 2 * RW_WIDTH:3 * RW_WIDTH]
    wa = zs[:, 3 * RW_WIDTH:3 * RW_WIDTH + DECAY_LORA + AAA_LORA]
    gd = zs[:, 3 * RW_WIDTH + DECAY_LORA + AAA_LORA:]

    lw = w0_ref[...] + _dot(jnp.tanh(wa).astype(BF), w2_ref[...])
    log_decay = -math.exp(-0.5) * _sigmoid(lw)
    a = _sigmoid(a0_ref[...] + _dot(wa.astype(BF), a2_ref[...]))
    g = _dot(_sigmoid(gd).astype(BF), g2_ref[...])
    kk = k * kk_ref[...]
    ss = _dot_exact_rhs(kk * kk, j_ref[...])
    kk = kk / jnp.maximum(jnp.sqrt(ss), 1e-12)
    kmod = k * (1.0 + (a - 1.0) * ka_ref[...])

    r_ref[0] = r
    lw_ref[0] = log_decay
    k_ref[0] = kmod
    v_ref[0] = v
    a_ref[0] = -kk
    b_ref[0] = kk * a
    g_ref[0] = g


def _rwkv_prep(zr, shift0, p, tt):
    b, t, _ = zr.shape
    zero_pad = jnp.zeros((DECAY_LORA, RW_WIDTH), F32)
    w2p = jnp.concatenate([p["w_w2"], zero_pad], axis=0).astype(BF)
    a2p = jnp.concatenate([zero_pad, p["a_a2"]], axis=0).astype(BF)
    tok = lambda w: pl.BlockSpec((1, tt, w), lambda i, j: (i, j, 0))
    par = lambda w: pl.BlockSpec((1, w), lambda i, j: (0, 0))
    mat = lambda r, c: pl.BlockSpec((r, c), lambda i, j: (0, 0))
    return pl.pallas_call(
        _rwkv_prep_kernel,
        grid=(b, t // tt),
        in_specs=[tok(RW_IN), pl.BlockSpec((1, 1, RW_IN), lambda i, j: (i, 0, 0)),
                  par(RW_IN), par(RW_WIDTH), par(RW_WIDTH), par(RW_WIDTH), par(RW_WIDTH),
                  mat(DECAY_LORA + AAA_LORA, RW_WIDTH), mat(DECAY_LORA + AAA_LORA, RW_WIDTH),
                  mat(GATE_LORA, RW_WIDTH), mat(RW_WIDTH, RW_WIDTH)],
        out_specs=[tok(RW_WIDTH)] * 7,
        out_shape=[jax.ShapeDtypeStruct((b, t, RW_WIDTH), F32)] * 7,
        scratch_shapes=[pltpu.VMEM((1, RW_IN), F32)],
        compiler_params=_cp(("parallel", "arbitrary")),
        name="rwkv_prep",
    )(zr, shift0[:, None, :], p["mu_shift"][None, :], p["w0"][None, :], p["a0"][None, :],
      p["k_k"][None, :], p["k_a"][None, :], w2p, a2p, p["g_g2"].astype(BF),
      _group_ones(RW_WIDTH, RW_HEAD_DIM))


_NN = (((1,), (0,)), ((), ()))


def _d3(a, b, dims=_NN):
    return _dg(a[0], b[0], dims) + _dg(a[0], b[1], dims) + _dg(a[1], b[0], dims)


def _rwkv_chunk_kernel(r_ref, lw_ref, k_ref, v_ref, a_ref, b_ref, g_ref, s0_ref,
                       lnw_ref, lnb_ref, rk_ref, y_ref, sT_ref, s_ref, *, c):
    n = RW_HEAD_DIM
    n_sub = r_ref.shape[1] // c
    heads = range(RW_HEADS)
    units = [(j, h) for j in range(n_sub) for h in heads]

    @pl.when(pl.program_id(1) == 0)
    def _():
        s_ref[...] = s0_ref[0]

    row = lax.broadcasted_iota(jnp.int32, (c, c), 0)
    col = lax.broadcasted_iota(jnp.int32, (c, c), 1)
    lower = row >= col
    strict = row > col
    eye_c = (row == col).astype(F32)
    tri = _split(lower.astype(F32))

    f32 = {}
    p_chunk = []
    for j in range(n_sub):
        rows = slice(j * c, (j + 1) * c)
        r, lw, k, v, a, b = (ref[0, rows, :] for ref in (r_ref, lw_ref, k_ref, v_ref, a_ref, b_ref))
        cum = _d3(tri, _split(lw))
        cum_last = cum[c - 1:c, :]
        pinv = jnp.exp(-cum)
        pc = jnp.exp(cum_last - cum)
        p_chunk.append(jnp.exp(cum_last))
        full = {"rt": r * jnp.exp(cum), "at": a * jnp.exp(cum - lw), "bt": b * pinv,
                "kt": k * pinv, "bh": b * pc, "kh": k * pc, "v": v, "rk": r * k * rk_ref[...]}
        for nm, x in full.items():
            f32.setdefault(nm, []).extend(x[:, h * n:(h + 1) * n] for h in heads)

    idx = range(len(units))
    bf = lambda xs: [x.astype(BF) for x in xs]
    at, bt, kt, bh, vv = (bf(f32[nm]) for nm in ("at", "bt", "kt", "bh", "v"))
    at_rt = bf([jnp.concatenate([f32["at"][u], f32["rt"][u]], axis=0) for u in idx])

    xb = [_dg(at_rt[u], bt[u], _NT) for u in idx]
    xk = [_dg(at_rt[u], kt[u], _NT) for u in idx]
    a_ab = [jnp.where(strict, x[:c], 0.0) for x in xb]
    a_ak = bf([jnp.where(strict, x[:c], 0.0) for x in xk])
    a_rb = bf([jnp.where(lower, x[c:], 0.0) for x in xb])
    a_rk = bf([jnp.where(lower, x[c:], 0.0) for x in xk])

    t_inv = [eye_c + x for x in a_ab]
    n_sq = int(math.log2(c)) - 1
    if n_sq > 0:
        pw = bf(a_ab)
        pw = [_dot(x, x) for x in pw]
        for it in range(n_sq):
            pw_b = bf(pw)
            if it + 1 < n_sq:
                both = [_dot(jnp.concatenate([t_inv[u].astype(BF), pw_b[u]], axis=0), pw_b[u])
                        for u in idx]
                t_inv = [t_inv[u] + both[u][:c] for u in idx]
                pw = [x[c:] for x in both]
            else:
                t_inv = [t_inv[u] + _dot(t_inv[u].astype(BF), pw_b[u]) for u in idx]
    t_b = bf(t_inv)
    akv = bf([_dot(a_ak[u], vv[u]) for u in idx])
    w1 = bf([_dot(t_b[u], at[u]) for u in idx])
    u0 = bf([_dot(t_b[u], akv[u]) for u in idx])

    krow = lax.broadcasted_iota(jnp.int32, (n, n), 0)
    kcol = lax.broadcasted_iota(jnp.int32, (n, n), 1)
    eye_n = krow == kcol
    m_mat = [_split(jnp.where(eye_n, p_chunk[j][:, h * n:(h + 1) * n], 0.0)
                    + _dg(w1[u], bh[u], _TN)) for u, (j, h) in enumerate(units)]
    n_mat = [_dg(u0[u], bh[u], _TN) + _d3(_split(f32["v"][u]), _split(f32["kh"][u]), _TN) for u in idx]
    g_mat = bf([f32["rt"][u] + _dot(a_rb[u], w1[u]) for u in idx])
    y0 = [_dot(a_rb[u], u0[u]) + _dot(a_rk[u], vv[u]) for u in idx]

    state = [s_ref[h] for h in heads]
    for j in range(n_sub):
        s_old = [_split(state[h]) for h in heads]
        ys = [_dg(g_mat[j * RW_HEADS + h], s_old[h][0], _NT) + y0[j * RW_HEADS + h] for h in heads]
        state = [_d3(s_old[h], m_mat[j * RW_HEADS + h]) + n_mat[j * RW_HEADS + h] for h in heads]
        yn, bonus = [], []
        for h in heads:
            u = j * RW_HEADS + h
            mu = jnp.mean(ys[h], axis=-1, keepdims=True)
            var = jnp.mean(jnp.square(ys[h] - mu), axis=-1, keepdims=True)
            yn.append((ys[h] - mu) * lax.rsqrt(var + GN_EPS))
            bonus.append(jnp.sum(f32["rk"][u], axis=-1, keepdims=True) * f32["v"][u])
        yn = jnp.concatenate(yn, axis=-1)
        bonus = jnp.concatenate(bonus, axis=-1)
        rows = slice(j * c, (j + 1) * c)
        y_ref[0, rows, :] = ((yn * lnw_ref[...] + lnb_ref[...] + bonus) * g_ref[0, rows, :]).astype(BF)
    for h in heads:
        s_ref[h] = state[h]

    @pl.when(pl.program_id(1) == pl.num_programs(1) - 1)
    def _():
        sT_ref[0] = s_ref[...]


def _rwkv_chunk(r, lw, k, v, a, b, g, s0, p, c, n_sub):
    bsz, t, _ = r.shape
    tok = pl.BlockSpec((1, c * n_sub, RW_WIDTH), lambda i, j: (i, j, 0))
    par = pl.BlockSpec((1, RW_WIDTH), lambda i, j: (0, 0))
    st = pl.BlockSpec((1, RW_HEADS, RW_HEAD_DIM, RW_HEAD_DIM), lambda i, j: (i, 0, 0, 0))
    return pl.pallas_call(
        functools.partial(_rwkv_chunk_kernel, c=c),
        grid=(bsz, t // (c * n_sub)),
        in_specs=[tok] * 7 + [st, par, par, par],
        out_specs=[tok, st],
        out_shape=[jax.ShapeDtypeStruct((bsz, t, RW_WIDTH), BF),
                   jax.ShapeDtypeStruct((bsz, RW_HEADS, RW_HEAD_DIM, RW_HEAD_DIM), F32)],
        scratch_shapes=[pltpu.VMEM((RW_HEADS, RW_HEAD_DIM, RW_HEAD_DIM), F32)],
        compiler_params=_cp(("parallel", "arbitrary")),
        name="rwkv_chunk",
    )(r, lw, k, v, a, b, g, s0, p["ln_x_w"][None, :], p["ln_x_b"][None, :],
      p["r_k"].reshape(1, RW_WIDTH))


def _lambda_full(lq1, lk1, lq2, lk2):
    s1 = jnp.sum(lq1 * lk1, axis=-1, keepdims=True)
    s2 = jnp.sum(lq2 * lk2, axis=-1, keepdims=True)
    return jnp.exp(s1) - jnp.exp(s2) + LAMBDA_INIT


def _subln(o, g):
    y = o * lax.rsqrt(jnp.mean(o * o, axis=-1, keepdims=True) + NORM_EPS)
    return y * g * (1.0 - LAMBDA_INIT)


def _prompt_attn_kernel(qi_ref, ki_ref, q_ref, k_ref, v_ref, lq1, lk1, lq2, lk2, sg_ref,
                        o_ref, qm_ref, m_ref, l_ref, acc_ref):
    p = pl.program_id(2)
    qi = qi_ref[p]
    ki = ki_ref[p]
    tq = q_ref.shape[1]
    tk = k_ref.shape[1]

    @pl.when(ki == 0)
    def _():
        q = q_ref[0]
        lane = lax.broadcasted_iota(jnp.int32, q.shape, 1)
        zero = jnp.zeros_like(q)
        qm_ref[0:tq] = jnp.where(lane < DA_HEAD_DIM, q, zero)
        qm_ref[tq:2 * tq] = jnp.where(lane >= DA_HEAD_DIM, q, zero)
        m_ref[...] = jnp.full(m_ref.shape, -jnp.inf, F32)
        l_ref[...] = jnp.zeros(l_ref.shape, F32)
        acc_ref[...] = jnp.zeros(acc_ref.shape, F32)

    def step(masked):
        s = _dg(qm_ref[...], k_ref[0], _NT)
        if masked:
            rowi = lax.broadcasted_iota(jnp.int32, (2 * tq, tk), 0)
            coli = lax.broadcasted_iota(jnp.int32, (2 * tq, tk), 1)
            s = jnp.where(coli <= jnp.where(rowi >= tq, rowi - tq, rowi), s, -jnp.inf)
        m_old = m_ref[...]
        m_new = jnp.maximum(m_old, jnp.max(s, axis=-1, keepdims=True))
        alpha = jnp.exp2(m_old - m_new)
        pr = jnp.exp2(s - jnp.concatenate([m_new] * (tk // 128), axis=-1))
        l_ref[...] = alpha * l_ref[...] + jnp.sum(pr, axis=-1, keepdims=True)
        acc_ref[...] = alpha * acc_ref[...] + _dot(pr.astype(BF), v_ref[0])
        m_ref[...] = m_new

    @pl.when(ki < qi)
    def _():
        step(False)

    @pl.when(ki == qi)
    def _():
        step(True)
        lam = _lambda_full(lq1[...], lk1[...], lq2[...], lk2[...])
        o = acc_ref[...] / l_ref[...]
        o_ref[0] = _subln(o[0:tq] - lam * o[tq:2 * tq], sg_ref[...]).astype(BF)


def _prompt_attention(q, kb, vb, p, tq):
    bsz, t, _ = q.shape
    nq = t // tq
    qi = np.concatenate([np.full(i + 1, i) for i in range(nq)]).astype(np.int32)
    ki = np.concatenate([np.arange(i + 1) for i in range(nq)]).astype(np.int32)
    lam_spec = pl.BlockSpec((1, DA_HEAD_DIM), lambda b, h, s, qt, kt: (0, 0))
    grid_spec = pltpu.PrefetchScalarGridSpec(
        num_scalar_prefetch=2,
        grid=(bsz, DA_HEADS, len(qi)),
        in_specs=[pl.BlockSpec((1, tq, DA_V_DIM), lambda b, h, s, qt, kt: (b, qt[s], h)),
                  pl.BlockSpec((1, tq, DA_V_DIM), lambda b, h, s, qt, kt: (b, kt[s], h)),
                  pl.BlockSpec((1, tq, DA_V_DIM), lambda b, h, s, qt, kt: (b, kt[s], h)),
                  lam_spec, lam_spec, lam_spec, lam_spec,
                  pl.BlockSpec((1, DA_V_DIM), lambda b, h, s, qt, kt: (0, 0))],
        out_specs=pl.BlockSpec((1, tq, DA_V_DIM), lambda b, h, s, qt, kt: (b, qt[s], h)),
        scratch_shapes=[pltpu.VMEM((2 * tq, DA_V_DIM), BF),
                        pltpu.VMEM((2 * tq, 128), F32),
                        pltpu.VMEM((2 * tq, 128), F32),
                        pltpu.VMEM((2 * tq, DA_V_DIM), F32)])
    return pl.pallas_call(
        _prompt_attn_kernel,
        grid_spec=grid_spec,
        out_shape=jax.ShapeDtypeStruct((bsz, t, DA_WIDTH), BF),
        compiler_params=_cp(("parallel", "parallel", "arbitrary")),
        name="prompt_attn",
    )(jnp.asarray(qi), jnp.asarray(ki), q, kb, vb,
      p["lambda_q1"][None, :], p["lambda_k1"][None, :], p["lambda_q2"][None, :],
      p["lambda_k2"][None, :], p["subln_g"][None, :])


def _make_sample_attn_kernel(pages):
    n_pairs = DA_HEADS * 2
    groups = 1

    def kernel(pt_ref, q_ref, kn_ref, vn_ref, lq1, lk1, lq2, lk2, sg_ref, *rest):
        k_refs = rest[:pages]
        v_refs = rest[pages:2 * pages]
        o_ref = rest[2 * pages]
        qall_ref, m_ref, l_ref, acc_ref = rest[2 * pages + 1:]
        j = pl.program_id(1)
        s_len = q_ref.shape[1]
        rows = n_pairs * s_len
        page_rows = PAGE_SIZE * DA_HEADS

        @pl.when(j == 0)
        def _():
            q = q_ref[0].astype(F32)
            lane = lax.broadcasted_iota(jnp.int32, (s_len, DA_V_DIM), 1)
            pieces = []
            for h in range(DA_HEADS):
                qh = q[:, h * DA_V_DIM:(h + 1) * DA_V_DIM]
                pieces.append(jnp.where(lane < DA_HEAD_DIM, qh, 0.0))
                pieces.append(jnp.where(lane >= DA_HEAD_DIM, qh, 0.0))
            qall_ref[...] = jnp.concatenate(pieces, axis=0).astype(BF)
            m_ref[...] = jnp.full(m_ref.shape, -jnp.inf, F32)
            l_ref[...] = jnp.zeros(l_ref.shape, F32)
            acc_ref[...] = jnp.zeros(acc_ref.shape, F32)

        def partial_softmax(s, vals):
            m = jnp.max(s, axis=-1, keepdims=True)
            pr = jnp.exp2(s - m)
            return m, jnp.sum(pr, axis=-1, keepdims=True), _dot(pr.astype(BF), vals)

        def update(parts):
            m_old = m_ref[...]
            m_new = m_old
            for m, _, _ in parts:
                m_new = jnp.maximum(m_new, m)
            alpha = jnp.exp2(m_old - m_new)
            l_new = alpha * l_ref[...]
            acc = alpha * acc_ref[...]
            for m, l_part, acc_part in parts:
                w = jnp.exp2(m - m_new)
                l_new = l_new + w * l_part
                acc = acc + w * acc_part
            l_ref[...] = l_new
            acc_ref[...] = acc
            m_ref[...] = m_new

        qall = qall_ref[...]
        flat = lambda ref: ref[...].reshape(page_rows, DA_V_DIM).astype(BF)
        per_group = pages // groups
        parts = []
        for g in range(groups):
            sel = slice(g * per_group, (g + 1) * per_group)
            s_g = jnp.concatenate([_dg(qall, flat(kr), _NT) for kr in k_refs[sel]], axis=-1)
            v_g = jnp.concatenate([flat(vr) for vr in v_refs[sel]], axis=0)
            row_head = lax.broadcasted_iota(jnp.int32, s_g.shape, 0) // (2 * s_len)
            key_head = lax.broadcasted_iota(jnp.int32, s_g.shape, 1) % DA_HEADS
            parts.append(partial_softmax(jnp.where(row_head == key_head, s_g, -jnp.inf), v_g))
        update(parts)

        @pl.when(j == pl.num_programs(1) - 1)
        def _():
            kn = kn_ref[0].astype(F32)
            vn = vn_ref[0].astype(F32)
            by_head = lambda x: jnp.concatenate(
                [x[:, h * DA_V_DIM:(h + 1) * DA_V_DIM] for h in range(DA_HEADS)], axis=0).astype(BF)
            s = _dg(qall, by_head(kn), _NT)
            r = lax.broadcasted_iota(jnp.int32, s.shape, 0)
            c = lax.broadcasted_iota(jnp.int32, s.shape, 1)
            keep = (c // s_len == r // (2 * s_len)) & (c % s_len <= r % s_len)
            update([partial_softmax(jnp.where(keep, s, -jnp.inf), by_head(vn))])
            lam = _lambda_full(lq1[...], lk1[...], lq2[...], lk2[...])
            o = acc_ref[...] / l_ref[...]
            heads = []
            for h in range(DA_HEADS):
                o1 = o[(2 * h) * s_len:(2 * h + 1) * s_len]
                o2 = o[(2 * h + 1) * s_len:(2 * h + 2) * s_len]
                heads.append(_subln(o1 - lam * o2, sg_ref[...]))
            o_ref[0] = jnp.concatenate(heads, axis=-1).astype(BF)

    return kernel


def _sample_attention(q, kb, vb, cache_k, cache_v, page_table, p, pages):
    bsz, s_len, _ = q.shape
    n_pages = page_table.shape[1]
    rows = DA_HEADS * 2 * s_len
    tok = pl.BlockSpec((1, s_len, DA_WIDTH), lambda b, j, pt: (b, 0, 0))
    lam_spec = pl.BlockSpec((1, DA_HEAD_DIM), lambda b, j, pt: (0, 0))

    def page_spec(i):
        return pl.BlockSpec((None, None, PAGE_SIZE, DA_HEADS, DA_V_DIM),
                            lambda b, j, pt: (0, pt[b * n_pages + j * pages + i], 0, 0, 0))

    grid_spec = pltpu.PrefetchScalarGridSpec(
        num_scalar_prefetch=1,
        grid=(bsz, n_pages // pages),
        in_specs=[tok, tok, tok, lam_spec, lam_spec, lam_spec, lam_spec,
                  pl.BlockSpec((1, DA_V_DIM), lambda b, j, pt: (0, 0))]
                 + [page_spec(i) for i in range(pages)] * 2,
        out_specs=tok,
        scratch_shapes=[pltpu.VMEM((rows, DA_V_DIM), BF),
                        pltpu.VMEM((rows, 1), F32),
                        pltpu.VMEM((rows, 1), F32),
                        pltpu.VMEM((rows, DA_V_DIM), F32)])
    return pl.pallas_call(
        _make_sample_attn_kernel(pages),
        grid_spec=grid_spec,
        out_shape=jax.ShapeDtypeStruct((bsz, s_len, DA_WIDTH), BF),
        compiler_params=_cp(("parallel", "arbitrary")),
        name="sample_attn",
    )(page_table.reshape(-1), q, kb, vb,
      p["lambda_q1"][None, :], p["lambda_k1"][None, :], p["lambda_q2"][None, :],
      p["lambda_k2"][None, :], p["subln_g"][None, :], *([cache_k] * pages), *([cache_v] * pages))


def _merge_kernel(ya_ref, ob_ref, zg_ref, x_ref, wa_ref, wb_ref, wo_ref, g2_ref,
                  wr_hi_ref, wr_lo_ref, br_ref, h_ref, hn_ref, te_ref, tg_ref):
    zg = zg_ref[...]
    m = (_sigmoid(zg[:, :D_MODEL]) * _dot(ya_ref[...], wa_ref[...])
         + _sigmoid(zg[:, D_MODEL:]) * _dot(ob_ref[...], wb_ref[...]))
    h = x_ref[...] + _dot(m.astype(BF), wo_ref[...])
    h_ref[...] = h
    hn = h * lax.rsqrt(jnp.mean(h * h, axis=-1, keepdims=True) + NORM_EPS) * g2_ref[...]
    tm = hn.shape[0]
    for j in range(ROW_TILES):
        hn_ref[pl.ds(j, tm, stride=ROW_TILES), :] = hn[:, j * 128:(j + 1) * 128]

    hh, hl = _split(hn)
    logits = (_dg(wr_hi_ref[...], hh, _NT) + _dg(wr_hi_ref[...], hl, _NT)
              + _dg(wr_lo_ref[...], hh, _NT)) + br_ref[...]
    eidx = lax.broadcasted_iota(jnp.int32, logits.shape, 0)
    vals, idxs = [], []
    for _ in range(TOP_K):
        mx = jnp.max(logits, axis=0, keepdims=True)
        am = jnp.min(jnp.where(logits == mx, eidx, N_EXPERTS), axis=0, keepdims=True)
        vals.append(mx)
        idxs.append(am)
        logits = jnp.where(eidx == am, -jnp.inf, logits)
    ex = [jnp.exp(vv - vals[0]) for vv in vals]
    den = ex[0] + ex[1] + ex[2] + ex[3]
    te_ref[...] = jnp.concatenate(idxs, axis=0)
    tg_ref[...] = jnp.concatenate([e / den for e in ex], axis=0)


def _merge(ya, ob, zg, x2, p, tm):
    n = x2.shape[0]
    wr_t = p["w_router"].T
    wr_hi = wr_t.astype(BF)
    wr_lo = (wr_t - wr_hi.astype(F32)).astype(BF)
    row = lambda w: pl.BlockSpec((tm, w), lambda i: (i, 0))
    mat = lambda r, c: pl.BlockSpec((r, c), lambda i: (0, 0))
    colblk = pl.BlockSpec((TOP_K, tm), lambda i: (0, i))
    return pl.pallas_call(
        _merge_kernel,
        grid=(n // tm,),
        in_specs=[row(RW_WIDTH), row(DA_WIDTH), row(2 * D_MODEL), row(D_MODEL),
                  mat(RW_WIDTH, D_MODEL), mat(DA_WIDTH, D_MODEL), mat(D_MODEL, D_MODEL),
                  mat(1, D_MODEL), mat(N_EXPERTS, D_MODEL), mat(N_EXPERTS, D_MODEL),
                  mat(N_EXPERTS, 1)],
        out_specs=[row(D_MODEL), pl.BlockSpec((tm * ROW_TILES, 128), lambda i: (i, 0)),
                   colblk, colblk],
        out_shape=[jax.ShapeDtypeStruct((n, D_MODEL), F32),
                   jax.ShapeDtypeStruct((n * ROW_TILES, 128), F32),
                   jax.ShapeDtypeStruct((TOP_K, n), jnp.int32),
                   jax.ShapeDtypeStruct((TOP_K, n), F32)],
        compiler_params=_cp(("parallel",)),
        name="merge_router",
    )(ya, ob, zg, x2, p["w_a"].astype(BF), p["w_b"].astype(BF), p["w_o"].astype(BF),
      p["norm2_g"][None, :], wr_hi, wr_lo, p["b_router"][:, None])


def _experts_kernel(ib_ref, ie_ref, ilo_ref, ihi_ref, ifirst_ref, ilast_ref, ifused_ref,
                    idx_hbm, x_hbm, wu_ref, bu_ref, wd_ref, bd_ref, out_hbm,
                    idx_smem, idx_sem, xbuf, gsem, xb_ref, acc_ref, obuf, ssem):
    i = pl.program_id(0)
    bm = MOE_ROWS
    slab = bm * ROW_TILES
    n_blocks = idx_hbm.shape[0]
    blk = ib_ref[i]
    s2 = blk % 2

    def idx_copy(b):
        slot = b % IDX_SLOTS
        return pltpu.make_async_copy(idx_hbm.at[b], idx_smem.at[pl.ds(slot * 2 * bm, 2 * bm)],
                                     idx_sem.at[slot])

    def tile_rows(row):
        return pl.ds(pl.multiple_of(row * ROW_TILES, ROW_TILES), ROW_TILES)

    def gather_row(b, r):
        tok = idx_smem[(b % IDX_SLOTS) * 2 * bm + r]
        return pltpu.make_async_copy(x_hbm.at[tile_rows(tok)], xbuf.at[tile_rows((b % 2) * bm + r)],
                                     gsem.at[b % 2])

    def scatter_row(b, r):
        dst = idx_smem[(b % IDX_SLOTS) * 2 * bm + bm + r]
        return pltpu.make_async_copy(obuf.at[tile_rows((b % 2) * bm + r)], out_hbm.at[tile_rows(dst)],
                                     ssem.at[b % 2])

    def issue(copy, b, unrolled):
        if unrolled:
            for r in range(bm):
                copy(b, r).start()
        else:
            def body(r, carry):
                copy(b, r).start()
                return carry
            lax.fori_loop(0, bm, body, 0, unroll=8)

    def block_rows(slot2):
        return pl.ds(pl.multiple_of(slot2 * slab, slab), slab)

    def wait_gather(slot2):
        pltpu.make_async_copy(x_hbm.at[pl.ds(0, slab)], xbuf.at[block_rows(slot2)], gsem.at[slot2]).wait()

    def wait_scatter(slot2):
        pltpu.make_async_copy(obuf.at[block_rows(slot2)], out_hbm.at[pl.ds(0, slab)], ssem.at[slot2]).wait()

    def load_tokens():
        base = s2 * slab
        x = jnp.concatenate([xbuf[pl.ds(base + j, bm, stride=ROW_TILES), :] for j in range(ROW_TILES)],
                            axis=-1)
        return x.astype(BF)

    def mlp(x):
        h = _dot(x, wu_ref[...]) + bu_ref[...]
        hg = jnp.minimum(h[:, :D_FF], SWIGLU_LIMIT)
        hl = jnp.clip(h[:, D_FF:], -SWIGLU_LIMIT, SWIGLU_LIMIT)
        act = hg * _sigmoid(SWIGLU_ALPHA * hg) * (hl + 1.0)
        return _dot(act.astype(BF), wd_ref[...]) + bd_ref[...]

    def store_rows(y):
        base = s2 * slab
        for j in range(ROW_TILES):
            obuf[pl.ds(base + j, bm, stride=ROW_TILES), :] = y[:, j * 128:(j + 1) * 128]

    first = ifirst_ref[i] == 1
    fused = ifused_ref[i] == 1

    @pl.when(i == 0)
    def _():
        idx_copy(0).start()
        idx_copy(0).wait()
        issue(gather_row, 0, False)
        if n_blocks > 1:
            idx_copy(1).start()

    @pl.when(first & (blk + 2 < n_blocks))
    def _():
        idx_copy(blk + 2).start()

    @pl.when(fused)
    def _():
        wait_gather(s2)
        x = load_tokens()
        issue(scatter_row, blk - 1, True)
        idx_copy(blk + 1).wait()
        issue(gather_row, blk + 1, True)
        y = mlp(x)
        wait_scatter(s2)
        store_rows(y)

    @pl.when(jnp.logical_not(fused))
    def _():
        @pl.when(first)
        def _():
            wait_gather(s2)

            @pl.when(blk >= 1)
            def _():
                issue(scatter_row, blk - 1, False)

            @pl.when(blk + 1 < n_blocks)
            def _():
                idx_copy(blk + 1).wait()
                issue(gather_row, blk + 1, False)

            xb_ref[...] = load_tokens()
            acc_ref[...] = jnp.zeros(acc_ref.shape, F32)

        lo = ilo_ref[i]
        hi = ihi_ref[i]

        @pl.when(hi > lo)
        def _():
            y = mlp(xb_ref[...])
            rowi = lax.broadcasted_iota(jnp.int32, (bm, 1), 0)
            acc_ref[...] += jnp.where((rowi >= lo) & (rowi < hi), y, 0.0)

        @pl.when(ilast_ref[i] == 1)
        def _():
            @pl.when(blk >= 2)
            def _():
                wait_scatter(s2)

            store_rows(acc_ref[...])

    @pl.when(i == pl.num_programs(0) - 1)
    def _():
        issue(scatter_row, n_blocks - 1, False)
        wait_scatter((n_blocks - 1) % 2)
        if n_blocks > 1:
            wait_scatter((n_blocks - 2) % 2)


def _experts(hn_rows, items, idx, w_up, b_up, w_down, b_down):
    bm = MOE_ROWS
    n_blocks = idx.shape[0]
    n_items = items[0].shape[0]
    wspec = lambda r, c: pl.BlockSpec((None, r, c), lambda i, ib, ie, *_: (ie[i], 0, 0))
    grid_spec = pltpu.PrefetchScalarGridSpec(
        num_scalar_prefetch=7,
        grid=(n_items,),
        in_specs=[pl.BlockSpec(memory_space=pl.ANY), pl.BlockSpec(memory_space=pl.ANY),
                  wspec(D_MODEL, 2 * D_FF), wspec(1, 2 * D_FF), wspec(D_FF, D_MODEL), wspec(1, D_MODEL)],
        out_specs=pl.BlockSpec(memory_space=pl.ANY),
        scratch_shapes=[pltpu.SMEM((IDX_SLOTS * 2 * bm,), jnp.int32),
                        pltpu.SemaphoreType.DMA((IDX_SLOTS,)),
                        pltpu.VMEM((2 * bm * ROW_TILES, 128), F32),
                        pltpu.SemaphoreType.DMA((2,)),
                        pltpu.VMEM((bm, D_MODEL), BF),
                        pltpu.VMEM((bm, D_MODEL), F32),
                        pltpu.VMEM((2 * bm * ROW_TILES, 128), F32),
                        pltpu.SemaphoreType.DMA((2,))])
    return pl.pallas_call(
        _experts_kernel,
        grid_spec=grid_spec,
        out_shape=jax.ShapeDtypeStruct((n_blocks * bm * ROW_TILES, 128), F32),
        compiler_params=_cp(("arbitrary",)),
        name="experts",
    )(*items, idx, hn_rows, w_up, b_up[:, None, :], w_down, b_down[:, None, :])


def _combine_kernel(h_ref, g_ref, *rest):
    e_refs = rest[:TOP_K]
    o_ref = rest[TOP_K]
    tm = h_ref.shape[0]
    g = g_ref[...]
    for j in range(ROW_TILES):
        acc = h_ref[:, j * 128:(j + 1) * 128]
        for kk in range(TOP_K):
            acc = acc + g[:, kk:kk + 1] * e_refs[kk][pl.ds(j, tm, stride=ROW_TILES), :]
        o_ref[:, j * 128:(j + 1) * 128] = acc


def _combine(h, gates_t, expert_rows, tm):
    n = h.shape[0]
    nt = n // tm
    slot = lambda kk: pl.BlockSpec((tm * ROW_TILES, 128), lambda i: (kk * nt + i, 0))
    return pl.pallas_call(
        _combine_kernel,
        grid=(nt,),
        in_specs=[pl.BlockSpec((tm, D_MODEL), lambda i: (i, 0)),
                  pl.BlockSpec((tm, TOP_K), lambda i: (i, 0))]
                 + [slot(kk) for kk in range(TOP_K)],
        out_specs=pl.BlockSpec((tm, D_MODEL), lambda i: (i, 0)),
        out_shape=jax.ShapeDtypeStruct((n, D_MODEL), F32),
        compiler_params=_cp(("parallel",)),
        name="combine",
    )(h, gates_t, *([expert_rows] * TOP_K))


def _route(top_e):
    n = top_e.shape[1]
    nk = n * TOP_K
    bm = MOE_ROWS
    n_blocks = nk // bm
    n_items = n_blocks + N_EXPERTS
    flat_e = top_e.reshape(-1)
    _, sorted_slot = lax.sort((flat_e, jnp.arange(nk, dtype=jnp.int32)), num_keys=1)
    experts = jnp.arange(N_EXPERTS, dtype=jnp.int32)
    counts = jnp.sum((flat_e[None, :] == experts[:, None]).astype(jnp.int32), axis=1)
    end = jnp.cumsum(counts)
    start = end - counts
    first_blk = start // bm
    n_e = jnp.where(counts > 0, (end - 1) // bm - first_blk + 1, 0)
    cum = jnp.cumsum(n_e)
    off = cum - n_e
    total = cum[-1]
    it = jnp.arange(n_items, dtype=jnp.int32)
    valid = it < total
    e_of = lambda t: jnp.minimum(jnp.sum((cum[None, :] <= t[:, None]).astype(jnp.int32), axis=1),
                                 N_EXPERTS - 1)
    e_i = jnp.where(valid, e_of(it), e_of(total[None] - 1)[0])
    blk = jnp.where(valid, first_blk[e_i] + it - off[e_i], n_blocks - 1)
    lo = jnp.where(valid, jnp.clip(start[e_i] - blk * bm, 0, bm), 0)
    hi = jnp.where(valid, jnp.clip(end[e_i] - blk * bm, 0, bm), 0)
    prev_blk = jnp.concatenate([jnp.full((1,), -1, jnp.int32), blk[:-1]])
    next_blk = jnp.concatenate([blk[1:], jnp.full((1,), -1, jnp.int32)])
    first = valid & (blk != prev_blk)
    last = valid & ((blk != next_blk) | (it == total - 1))
    fused = first & last & (blk >= 2) & (blk + 1 < n_blocks)
    items = tuple(x.astype(jnp.int32) for x in (blk, e_i, lo, hi, first, last, fused))
    idx = jnp.concatenate([(sorted_slot % n).reshape(n_blocks, bm), sorted_slot.reshape(n_blocks, bm)],
                          axis=1)
    return items, idx


def _moe(h, hn_rows, top_e, top_g, wts, tm):
    items, idx = _route(top_e)
    expert_rows = _experts(hn_rows, items, idx, *wts)
    return _combine(h, top_g.T, expert_rows, tm)


def _pick(n, pref):
    t = min(n, pref)
    assert n % t == 0
    return t


def _layer(x, pos, s0, shift0, p, w_in_bf, moe_wts, attend, chunk):
    b, t, _ = x.shape
    n = b * t
    tm = _pick(n, 256)
    tt = _pick(t, 256)
    x2 = x.reshape(n, D_MODEL)
    zr, zq, zk, zv, zvb, zg = _inproj(x2, p["norm1_g"][None, :], w_in_bf, tm)
    zr3 = zr.reshape(b, t, RW_IN)
    q, k, kb = _qk_prep(zq.reshape(b, t, DA_WIDTH), zk.reshape(b, t, DA_WIDTH),
                        p["q_norm_g"], p["k_norm_g"], pos, tt)
    r, lw, km, v, a, bb, g = _rwkv_prep(zr3, shift0, p, tt)
    ya, s_t = _rwkv_chunk(r, lw, km, v, a, bb, g, s0, p, chunk, _pick(t // chunk, 4))
    ob = attend(q, kb, zvb.reshape(b, t, DA_WIDTH))
    h, hn, top_e, top_g = _merge(ya.reshape(n, RW_WIDTH), ob.reshape(n, DA_WIDTH), zg, x2, p, tm)
    y = _moe(h, hn, top_e, top_g, moe_wts, tm)
    return (y.reshape(b, t, D_MODEL), k.reshape(b, t, DA_HEADS, 2 * DA_HEAD_DIM),
            zv.reshape(b, t, DA_HEADS, DA_V_DIM), s_t, zr3[:, -1])


def kernel(x_prompt, x_sample, cache_k, cache_v, page_table, state_wkv, state_shift, norm1_g, w_in, mu_shift, w0, w_w2, a0, a_a2, g_g2, k_k, k_a, r_k, ln_x_w, ln_x_b, q_norm_g, k_norm_g, lambda_q1, lambda_k1, lambda_q2, lambda_k2, subln_g, w_a, w_b, w_o, norm2_g, w_router, b_router, w_up, b_up, w_down, b_down):
    names = ["norm1_g", "mu_shift", "w0", "w_w2", "a0", "a_a2", "g_g2", "k_k", "k_a", "r_k",
             "ln_x_w", "ln_x_b", "q_norm_g", "k_norm_g", "lambda_q1", "lambda_k1", "lambda_q2",
             "lambda_k2", "subln_g", "w_a", "w_b", "w_o", "norm2_g", "w_router", "b_router"]
    vals = [norm1_g, mu_shift, w0, w_w2, a0, a_a2, g_g2, k_k, k_a, r_k, ln_x_w, ln_x_b,
            q_norm_g, k_norm_g, lambda_q1, lambda_k1, lambda_q2, lambda_k2, subln_g,
            w_a, w_b, w_o, norm2_g, w_router, b_router]
    p = {nm: vv[0] for nm, vv in zip(names, vals)}
    w_in_bf = w_in[0].astype(BF)
    moe_wts = (w_up[0].astype(BF), b_up[0], w_down[0].astype(BF), b_down[0])

    bp, tp, _ = x_prompt.shape
    bs, ts, _ = x_sample.shape
    past = page_table.shape[1] * PAGE_SIZE

    attend_p = lambda q, kb, vb: _prompt_attention(q, kb, vb, p, _pick(tp, 1024))
    yp, kp, vp, wp, sp = _layer(
        x_prompt, jnp.arange(tp), jnp.zeros((bp, RW_HEADS, RW_HEAD_DIM, RW_HEAD_DIM), F32),
        jnp.zeros((bp, RW_IN), F32), p, w_in_bf, moe_wts, attend_p, _pick(tp, 64))

    attend_s = lambda q, kb, vb: _sample_attention(
        q, kb, vb, cache_k, cache_v, page_table, p, _pick(page_table.shape[1], 8))
    ys, ks, vs, ws, ss = _layer(
        x_sample, past + jnp.arange(ts), state_wkv[0], state_shift[0], p, w_in_bf, moe_wts,
        attend_s, ts)

    return (yp, ys, kp[None], vp[None], wp[None], sp[None],
            ks[None], vs[None], ws[None], ss[None])
```

```python
import functools
import math

import jax
import jax.numpy as jnp
import numpy as np
from jax import lax
from jax.experimental import pallas as pl
from jax.experimental.pallas import tpu as pltpu

F32 = jnp.float32
BF = jnp.bfloat16

D_MODEL = 1024
PAGE_SIZE = 128
RW_HEADS = 8
RW_HEAD_DIM = 64
RW_WIDTH = RW_HEADS * RW_HEAD_DIM
DECAY_LORA = 64
AAA_LORA = 64
GATE_LORA = 128
RW_IN = 3 * RW_WIDTH + DECAY_LORA + AAA_LORA + GATE_LORA
GN_EPS = 64e-5
DA_HEADS = 4
DA_HEAD_DIM = 64
DA_V_DIM = 2 * DA_HEAD_DIM
DA_WIDTH = DA_HEADS * DA_V_DIM
ROPE_DIM = DA_HEAD_DIM // 4
ROPE_THETA = 500000.0
N_IN = RW_IN + 3 * DA_WIDTH + 2 * D_MODEL
N_EXPERTS = 32
TOP_K = 4
D_FF = D_MODEL
SWIGLU_ALPHA = 1.702
SWIGLU_LIMIT = 7.0
NORM_EPS = 1e-5
LOG2_E = 1.4426950408889634
LAMBDA_INIT = 0.8 - 0.6 * math.exp(-0.3 * 0)

VMEM_LIMIT = 56 * 1024 * 1024
MOE_ROWS = 256
IDX_SLOTS = 4
ROW_TILES = D_MODEL // 128
assert ROW_TILES == 8

_NT = (((1,), (1,)), ((), ()))
_TN = (((0,), (0,)), ((), ()))


def _cp(sem, vmem=VMEM_LIMIT):
    return pltpu.CompilerParams(dimension_semantics=sem, vmem_limit_bytes=vmem)


def _dot(a, b):
    return jnp.dot(a, b, preferred_element_type=F32)


def _dg(a, b, dims):
    return lax.dot_general(a, b, dims, preferred_element_type=F32)


def _split(x):
    hi = x.astype(BF)
    lo = (x - hi.astype(F32)).astype(BF)
    return hi, lo


def _dot_exact_rhs(a, b_bf16):
    ah, al = _split(a)
    return _dot(ah, b_bf16) + _dot(al, b_bf16)


def _sigmoid(x):
    return 1.0 / (1.0 + jnp.exp(-x))


_SEGS = (0, RW_IN, RW_IN + DA_WIDTH, RW_IN + 2 * DA_WIDTH, RW_IN + 3 * DA_WIDTH, N_IN)


def _store_by_head(o_ref, x):
    rows = x.shape[0]
    for h in range(DA_HEADS):
        o_ref[pl.ds(h, rows, stride=DA_HEADS), :] = x[:, h * DA_V_DIM:(h + 1) * DA_V_DIM]


def _inproj_kernel(x_ref, g_ref, w_ref, zr_ref, zq_ref, zk_ref, zv_ref, zvb_ref, zg_ref):
    x = x_ref[...]
    xn = x * lax.rsqrt(jnp.mean(x * x, axis=-1, keepdims=True) + NORM_EPS) * g_ref[...]
    xb = xn.astype(BF)
    zr_ref[...] = _dot(xb, w_ref[:, _SEGS[0]:_SEGS[1]])
    zq_ref[...] = _dot(xb, w_ref[:, _SEGS[1]:_SEGS[2]])
    zk_ref[...] = _dot(xb, w_ref[:, _SEGS[2]:_SEGS[3]])
    zv = _dot(xb, w_ref[:, _SEGS[3]:_SEGS[4]])
    _store_by_head(zv_ref, zv)
    zvb_ref[...] = zv.astype(BF)
    zg_ref[...] = _dot(xb, w_ref[:, _SEGS[4]:_SEGS[5]])


def _inproj(x2, g, w_bf, tm):
    n = x2.shape[0]
    row = lambda w: pl.BlockSpec((tm, w), lambda i: (i, 0))
    return pl.pallas_call(
        _inproj_kernel,
        grid=(n // tm,),
        in_specs=[row(D_MODEL),
                  pl.BlockSpec((1, D_MODEL), lambda i: (0, 0)),
                  pl.BlockSpec((D_MODEL, N_IN), lambda i: (0, 0))],
        out_specs=[row(RW_IN), row(DA_WIDTH), row(DA_WIDTH),
                   pl.BlockSpec((tm * DA_HEADS, DA_V_DIM), lambda i: (i, 0)), row(DA_WIDTH),
                   row(2 * D_MODEL)],
        out_shape=[jax.ShapeDtypeStruct((n, RW_IN), F32),
                   jax.ShapeDtypeStruct((n, DA_WIDTH), F32),
                   jax.ShapeDtypeStruct((n, DA_WIDTH), F32),
                   jax.ShapeDtypeStruct((n * DA_HEADS, DA_V_DIM), F32),
                   jax.ShapeDtypeStruct((n, DA_WIDTH), BF),
                   jax.ShapeDtypeStruct((n, 2 * D_MODEL), F32)],
        compiler_params=_cp(("parallel",)),
        name="inproj",
    )(x2, g, w_bf)


def _group_ones(width, group):
    i = np.arange(width)
    return jnp.asarray((i[:, None] // group) == (i[None, :] // group), dtype=BF)


def _qk_prep_kernel(zq_ref, zk_ref, gq_ref, gk_ref, cos_ref, sa_ref, sb_ref, j_ref,
                    q_ref, k_ref, kb_ref):
    cos = jnp.concatenate([cos_ref[...]] * (DA_WIDTH // 128), axis=-1)
    sa = jnp.concatenate([sa_ref[...]] * (DA_WIDTH // 128), axis=-1)
    sb = jnp.concatenate([sb_ref[...]] * (DA_WIDTH // 128), axis=-1)

    def norm_rope(z, g):
        ms = _dot_exact_rhs(z * z, j_ref[...]) * (1.0 / DA_HEAD_DIM)
        y = z * lax.rsqrt(ms + NORM_EPS) * g
        half = ROPE_DIM // 2
        up = pltpu.roll(y, DA_WIDTH - half, axis=1)
        dn = pltpu.roll(y, half, axis=1)
        return y * cos + up * sa + dn * sb

    q = norm_rope(zq_ref[0], gq_ref[...])
    k = norm_rope(zk_ref[0], gk_ref[...])
    q_ref[0] = (q * (DA_HEAD_DIM ** -0.5 * LOG2_E)).astype(BF)
    _store_by_head(k_ref, k)
    kb_ref[0] = k.astype(BF)


def _rope_tables(pos):
    half = ROPE_DIM // 2
    inv = ROPE_THETA ** (-jnp.arange(half, dtype=F32) / half)
    ang = pos.astype(F32)[:, None] * inv[None, :]
    lane = np.arange(128) % DA_HEAD_DIM
    idx = lane % half
    c = jnp.cos(ang)[:, idx]
    s = jnp.sin(ang)[:, idx]
    in_rope = jnp.asarray(lane < ROPE_DIM)
    first = jnp.asarray(lane < half)
    second = jnp.asarray((lane >= half) & (lane < ROPE_DIM))
    cos_t = jnp.where(in_rope[None, :], c, 1.0)
    sa_t = jnp.where(first[None, :], -s, 0.0)
    sb_t = jnp.where(second[None, :], s, 0.0)
    return cos_t, sa_t, sb_t


def _qk_prep(zq, zk, gq, gk, pos, tt):
    b, t, _ = zq.shape
    cos_t, sa_t, sb_t = _rope_tables(pos)
    gq_t = jnp.tile(gq, DA_WIDTH // DA_HEAD_DIM)[None, :]
    gk_t = jnp.tile(gk, DA_WIDTH // DA_HEAD_DIM)[None, :]
    tok = pl.BlockSpec((1, tt, DA_WIDTH), lambda i, j: (i, j, 0))
    tab = pl.BlockSpec((tt, 128), lambda i, j: (j, 0))
    par = pl.BlockSpec((1, DA_WIDTH), lambda i, j: (0, 0))
    return pl.pallas_call(
        _qk_prep_kernel,
        grid=(b, t // tt),
        in_specs=[tok, tok, par, par, tab, tab, tab,
                  pl.BlockSpec((DA_WIDTH, DA_WIDTH), lambda i, j: (0, 0))],
        out_specs=[tok, pl.BlockSpec((tt * DA_HEADS, DA_V_DIM), lambda i, j: (i * (t // tt) + j, 0)), tok],
        out_shape=[jax.ShapeDtypeStruct((b, t, DA_WIDTH), BF),
                   jax.ShapeDtypeStruct((b * t * DA_HEADS, DA_V_DIM), F32),
                   jax.ShapeDtypeStruct((b, t, DA_WIDTH), BF)],
        compiler_params=_cp(("parallel", "parallel")),
        name="qk_prep",
    )(zq, zk, gq_t, gk_t, cos_t, sa_t, sb_t, _group_ones(DA_WIDTH, DA_HEAD_DIM))


def _rwkv_prep_kernel(zr_ref, sh_ref, mu_ref, w0_ref, a0_ref, kk_ref, ka_ref,
                      w2_ref, a2_ref, g2_ref, j_ref,
                      r_ref, lw_ref, k_ref, v_ref, a_ref, b_ref, g_ref, carry_ref):
    tt = zr_ref.shape[1]

    @pl.when(pl.program_id(1) == 0)
    def _():
        carry_ref[...] = sh_ref[0]

    z = zr_ref[0]
    rolled = pltpu.roll(z, 1, axis=0)
    first_row = lax.broadcasted_iota(jnp.int32, (tt, 1), 0) == 0
    zprev = jnp.where(first_row, carry_ref[...], rolled)
    carry_ref[...] = z[tt - 1:tt, :]
    zs = z + (zprev - z) * mu_ref[...]

    r = zs[:, 0:RW_WIDTH]
    k = zs[:, RW_WIDTH:2 * RW_WIDTH]
    v = zs[:, 2 * RW_WIDTH:3 * RW_WIDTH]
    wa = zs[:, 3 * RW_WIDTH:3 * RW_WIDTH + DECAY_LORA + AAA_LORA]
    gd = zs[:, 3 * RW_WIDTH + DECAY_LORA + AAA_LORA:]

    lw = w0_ref[...] + _dot(jnp.tanh(wa).astype(BF), w2_ref[...])
    log_decay = -math.exp(-0.5) * _sigmoid(lw)
    a = _sigmoid(a0_ref[...] + _dot(wa.astype(BF), a2_ref[...]))
    g = _dot(_sigmoid(gd).astype(BF), g2_ref[...])
    kk = k * kk_ref[...]
    ss = _dot_exact_rhs(kk * kk, j_ref[...])
    kk = kk / jnp.maximum(jnp.sqrt(ss), 1e-12)
    kmod = k * (1.0 + (a - 1.0) * ka_ref[...])

    r_ref[0] = r
    lw_ref[0] = log_decay
    k_ref[0] = kmod
    v_ref[0] = v
    a_ref[0] = -kk
    b_ref[0] = kk * a
    g_ref[0] = g


def _rwkv_prep(zr, shift0, p, tt):
    b, t, _ = zr.shape
    zero_pad = jnp.zeros((DECAY_LORA, RW_WIDTH), F32)
    w2p = jnp.concatenate([p["w_w2"], zero_pad], axis=0).astype(BF)
    a2p = jnp.concatenate([zero_pad, p["a_a2"]], axis=0).astype(BF)
    tok = lambda w: pl.BlockSpec((1, tt, w), lambda i, j: (i, j, 0))
    par = lambda w: pl.BlockSpec((1, w), lambda i, j: (0, 0))
    mat = lambda r, c: pl.BlockSpec((r, c), lambda i, j: (0, 0))
    return pl.pallas_call(
        _rwkv_prep_kernel,
        grid=(b, t // tt),
        in_specs=[tok(RW_IN), pl.BlockSpec((1, 1, RW_IN), lambda i, j: (i, 0, 0)),
                  par(RW_IN), par(RW_WIDTH), par(RW_WIDTH), par(RW_WIDTH), par(RW_WIDTH),
                  mat(DECAY_LORA + AAA_LORA, RW_WIDTH), mat(DECAY_LORA + AAA_LORA, RW_WIDTH),
                  mat(GATE_LORA, RW_WIDTH), mat(RW_WIDTH, RW_WIDTH)],
        out_specs=[tok(RW_WIDTH)] * 7,
        out_shape=[jax.ShapeDtypeStruct((b, t, RW_WIDTH), F32)] * 7,
        scratch_shapes=[pltpu.VMEM((1, RW_IN), F32)],
        compiler_params=_cp(("parallel", "arbitrary")),
        name="rwkv_prep",
    )(zr, shift0[:, None, :], p["mu_shift"][None, :], p["w0"][None, :], p["a0"][None, :],
      p["k_k"][None, :], p["k_a"][None, :], w2p, a2p, p["g_g2"].astype(BF),
      _group_ones(RW_WIDTH, RW_HEAD_DIM))


_NN = (((1,), (0,)), ((), ()))


def _d3(a, b, dims=_NN):
    return _dg(a[0], b[0], dims) + _dg(a[0], b[1], dims) + _dg(a[1], b[0], dims)


def _rwkv_chunk_kernel(r_ref, lw_ref, k_ref, v_ref, a_ref, b_ref, g_ref, s0_ref,
                       lnw_ref, lnb_ref, rk_ref, y_ref, sT_ref, s_ref, *, c):
    n = RW_HEAD_DIM
    n_sub = r_ref.shape[1] // c
    heads = range(RW_HEADS)
    units = [(j, h) for j in range(n_sub) for h in heads]

    @pl.when(pl.program_id(1) == 0)
    def _():
        s_ref[...] = s0_ref[0]

    row = lax.broadcasted_iota(jnp.int32, (c, c), 0)
    col = lax.broadcasted_iota(jnp.int32, (c, c), 1)
    lower = row >= col
    strict = row > col
    eye_c = (row == col).astype(F32)
    tri = _split(lower.astype(F32))

    f32 = {}
    p_chunk = []
    for j in range(n_sub):
        rows = slice(j * c, (j + 1) * c)
        r, lw, k, v, a, b = (ref[0, rows, :] for ref in (r_ref, lw_ref, k_ref, v_ref, a_ref, b_ref))
        cum = _d3(tri, _split(lw))
        cum_last = cum[c - 1:c, :]
        pinv = jnp.exp(-cum)
        pc = jnp.exp(cum_last - cum)
        p_chunk.append(jnp.exp(cum_last))
        full = {"rt": r * jnp.exp(cum), "at": a * jnp.exp(cum - lw), "bt": b * pinv,
                "kt": k * pinv, "bh": b * pc, "kh": k * pc, "v": v, "rk": r * k * rk_ref[...]}
        for nm, x in full.items():
            f32.setdefault(nm, []).extend(x[:, h * n:(h + 1) * n] for h in heads)

    idx = range(len(units))
    bf = lambda xs: [x.astype(BF) for x in xs]
    at, bt, kt, bh, vv = (bf(f32[nm]) for nm in ("at", "bt", "kt", "bh", "v"))
    at_rt = bf([jnp.concatenate([f32["at"][u], f32["rt"][u]], axis=0) for u in idx])

    xb = [_dg(at_rt[u], bt[u], _NT) for u in idx]
    xk = [_dg(at_rt[u], kt[u], _NT) for u in idx]
    a_ab = [jnp.where(strict, x[:c], 0.0) for x in xb]
    a_ak = bf([jnp.where(strict, x[:c], 0.0) for x in xk])
    a_rb = bf([jnp.where(lower, x[c:], 0.0) for x in xb])
    a_rk = bf([jnp.where(lower, x[c:], 0.0) for x in xk])

    t_inv = [eye_c + x for x in a_ab]
    n_sq = int(math.log2(c)) - 1
    if n_sq > 0:
        pw = bf(a_ab)
        pw = [_dot(x, x) for x in pw]
        for it in range(n_sq):
            pw_b = bf(pw)
            if it + 1 < n_sq:
                both = [_dot(jnp.concatenate([t_inv[u].astype(BF), pw_b[u]], axis=0), pw_b[u])
                        for u in idx]
                t_inv = [t_inv[u] + both[u][:c] for u in idx]
                pw = [x[c:] for x in both]
            else:
                t_inv = [t_inv[u] + _dot(t_inv[u].astype(BF), pw_b[u]) for u in idx]
    t_b = bf(t_inv)
    akv = bf([_dot(a_ak[u], vv[u]) for u in idx])
    w1 = bf([_dot(t_b[u], at[u]) for u in idx])
    u0 = bf([_dot(t_b[u], akv[u]) for u in idx])

    krow = lax.broadcasted_iota(jnp.int32, (n, n), 0)
    kcol = lax.broadcasted_iota(jnp.int32, (n, n), 1)
    eye_n = krow == kcol
    m_mat = [_split(jnp.where(eye_n, p_chunk[j][:, h * n:(h + 1) * n], 0.0)
                    + _dg(w1[u], bh[u], _TN)) for u, (j, h) in enumerate(units)]
    n_mat = [_dg(u0[u], bh[u], _TN) + _d3(_split(f32["v"][u]), _split(f32["kh"][u]), _TN) for u in idx]
    g_mat = bf([f32["rt"][u] + _dot(a_rb[u], w1[u]) for u in idx])
    y0 = [_dot(a_rb[u], u0[u]) + _dot(a_rk[u], vv[u]) for u in idx]

    state = [s_ref[h] for h in heads]
    for j in range(n_sub):
        s_old = [_split(state[h]) for h in heads]
        ys = [_dg(g_mat[j * RW_HEADS + h], s_old[h][0], _NT) + y0[j * RW_HEADS + h] for h in heads]
        state = [_d3(s_old[h], m_mat[j * RW_HEADS + h]) + n_mat[j * RW_HEADS + h] for h in heads]
        yn, bonus = [], []
        for h in heads:
            u = j * RW_HEADS + h
            mu = jnp.mean(ys[h], axis=-1, keepdims=True)
            var = jnp.mean(jnp.square(ys[h] - mu), axis=-1, keepdims=True)
            yn.append((ys[h] - mu) * lax.rsqrt(var + GN_EPS))
            bonus.append(jnp.sum(f32["rk"][u], axis=-1, keepdims=True) * f32["v"][u])
        yn = jnp.concatenate(yn, axis=-1)
        bonus = jnp.concatenate(bonus, axis=-1)
        rows = slice(j * c, (j + 1) * c)
        y_ref[0, rows, :] = ((yn * lnw_ref[...] + lnb_ref[...] + bonus) * g_ref[0, rows, :]).astype(BF)
    for h in heads:
        s_ref[h] = state[h]

    @pl.when(pl.program_id(1) == pl.num_programs(1) - 1)
    def _():
        sT_ref[0] = s_ref[...]


def _rwkv_chunk(r, lw, k, v, a, b, g, s0, p, c, n_sub):
    bsz, t, _ = r.shape
    tok = pl.BlockSpec((1, c * n_sub, RW_WIDTH), lambda i, j: (i, j, 0))
    par = pl.BlockSpec((1, RW_WIDTH), lambda i, j: (0, 0))
    st = pl.BlockSpec((1, RW_HEADS, RW_HEAD_DIM, RW_HEAD_DIM), lambda i, j: (i, 0, 0, 0))
    return pl.pallas_call(
        functools.partial(_rwkv_chunk_kernel, c=c),
        grid=(bsz, t // (c * n_sub)),
        in_specs=[tok] * 7 + [st, par, par, par],
        out_specs=[tok, st],
        out_shape=[jax.ShapeDtypeStruct((bsz, t, RW_WIDTH), BF),
                   jax.ShapeDtypeStruct((bsz, RW_HEADS, RW_HEAD_DIM, RW_HEAD_DIM), F32)],
        scratch_shapes=[pltpu.VMEM((RW_HEADS, RW_HEAD_DIM, RW_HEAD_DIM), F32)],
        compiler_params=_cp(("parallel", "arbitrary")),
        name="rwkv_chunk",
    )(r, lw, k, v, a, b, g, s0, p["ln_x_w"][None, :], p["ln_x_b"][None, :],
      p["r_k"].reshape(1, RW_WIDTH))


def _lambda_full(lq1, lk1, lq2, lk2):
    s1 = jnp.sum(lq1 * lk1, axis=-1, keepdims=True)
    s2 = jnp.sum(lq2 * lk2, axis=-1, keepdims=True)
    return jnp.exp(s1) - jnp.exp(s2) + LAMBDA_INIT


def _subln(o, g):
    y = o * lax.rsqrt(jnp.mean(o * o, axis=-1, keepdims=True) + NORM_EPS)
    return y * g * (1.0 - LAMBDA_INIT)


def _prompt_attn_kernel(qi_ref, ki_ref, q_ref, k_ref, v_ref, lq1, lk1, lq2, lk2, sg_ref,
                        o_ref, qm_ref, m_ref, l_ref, acc_ref):
    p = pl.program_id(2)
    qi = qi_ref[p]
    ki = ki_ref[p]
    tq = q_ref.shape[1]
    tk = k_ref.shape[1]

    @pl.when(ki == 0)
    def _():
        q = q_ref[0]
        lane = lax.broadcasted_iota(jnp.int32, q.shape, 1)
        zero = jnp.zeros_like(q)
        qm_ref[0:tq] = jnp.where(lane < DA_HEAD_DIM, q, zero)
        qm_ref[tq:2 * tq] = jnp.where(lane >= DA_HEAD_DIM, q, zero)
        m_ref[...] = jnp.full(m_ref.shape, -jnp.inf, F32)
        l_ref[...] = jnp.zeros(l_ref.shape, F32)
        acc_ref[...] = jnp.zeros(acc_ref.shape, F32)

    def step(masked):
        s = _dg(qm_ref[...], k_ref[0], _NT)
        if masked:
            rowi = lax.broadcasted_iota(jnp.int32, (2 * tq, tk), 0)
            coli = lax.broadcasted_iota(jnp.int32, (2 * tq, tk), 1)
            s = jnp.where(coli <= jnp.where(rowi >= tq, rowi - tq, rowi), s, -jnp.inf)
        m_old = m_ref[...]
        m_new = jnp.maximum(m_old, jnp.max(s, axis=-1, keepdims=True))
        alpha = jnp.exp2(m_old - m_new)
        pr = jnp.exp2(s - jnp.concatenate([m_new] * (tk // 128), axis=-1))
        l_ref[...] = alpha * l_ref[...] + jnp.sum(pr, axis=-1, keepdims=True)
        acc_ref[...] = alpha * acc_ref[...] + _dot(pr.astype(BF), v_ref[0])
        m_ref[...] = m_new

    @pl.when(ki < qi)
    def _():
        step(False)

    @pl.when(ki == qi)
    def _():
        step(True)
        lam = _lambda_full(lq1[...], lk1[...], lq2[...], lk2[...])
        o = acc_ref[...] / l_ref[...]
        o_ref[0] = _subln(o[0:tq] - lam * o[tq:2 * tq], sg_ref[...]).astype(BF)


def _prompt_attention(q, kb, vb, p, tq):
    bsz, t, _ = q.shape
    nq = t // tq
    qi = np.concatenate([np.full(i + 1, i) for i in range(nq)]).astype(np.int32)
    ki = np.concatenate([np.arange(i + 1) for i in range(nq)]).astype(np.int32)
    lam_spec = pl.BlockSpec((1, DA_HEAD_DIM), lambda b, h, s, qt, kt: (0, 0))
    grid_spec = pltpu.PrefetchScalarGridSpec(
        num_scalar_prefetch=2,
        grid=(bsz, DA_HEADS, len(qi)),
        in_specs=[pl.BlockSpec((1, tq, DA_V_DIM), lambda b, h, s, qt, kt: (b, qt[s], h)),
                  pl.BlockSpec((1, tq, DA_V_DIM), lambda b, h, s, qt, kt: (b, kt[s], h)),
                  pl.BlockSpec((1, tq, DA_V_DIM), lambda b, h, s, qt, kt: (b, kt[s], h)),
                  lam_spec, lam_spec, lam_spec, lam_spec,
                  pl.BlockSpec((1, DA_V_DIM), lambda b, h, s, qt, kt: (0, 0))],
        out_specs=pl.BlockSpec((1, tq, DA_V_DIM), lambda b, h, s, qt, kt: (b, qt[s], h)),
        scratch_shapes=[pltpu.VMEM((2 * tq, DA_V_DIM), BF),
                        pltpu.VMEM((2 * tq, 128), F32),
                        pltpu.VMEM((2 * tq, 128), F32),
                        pltpu.VMEM((2 * tq, DA_V_DIM), F32)])
    return pl.pallas_call(
        _prompt_attn_kernel,
        grid_spec=grid_spec,
        out_shape=jax.ShapeDtypeStruct((bsz, t, DA_WIDTH), BF),
        compiler_params=_cp(("parallel", "parallel", "arbitrary")),
        name="prompt_attn",
    )(jnp.asarray(qi), jnp.asarray(ki), q, kb, vb,
      p["lambda_q1"][None, :], p["lambda_k1"][None, :], p["lambda_q2"][None, :],
      p["lambda_k2"][None, :], p["subln_g"][None, :])


def _make_sample_attn_kernel(pages):
    n_pairs = DA_HEADS * 2
    groups = 1

    def kernel(pt_ref, q_ref, kn_ref, vn_ref, lq1, lk1, lq2, lk2, sg_ref, *rest):
        k_refs = rest[:pages]
        v_refs = rest[pages:2 * pages]
        o_ref = rest[2 * pages]
        qall_ref, m_ref, l_ref, acc_ref = rest[2 * pages + 1:]
        j = pl.program_id(1)
        s_len = q_ref.shape[1]
        rows = n_pairs * s_len
        page_rows = PAGE_SIZE * DA_HEADS

        @pl.when(j == 0)
        def _():
            q = q_ref[0].astype(F32)
            lane = lax.broadcasted_iota(jnp.int32, (s_len, DA_V_DIM), 1)
            pieces = []
            for h in range(DA_HEADS):
                qh = q[:, h * DA_V_DIM:(h + 1) * DA_V_DIM]
                pieces.append(jnp.where(lane < DA_HEAD_DIM, qh, 0.0))
                pieces.append(jnp.where(lane >= DA_HEAD_DIM, qh, 0.0))
            qall_ref[...] = jnp.concatenate(pieces, axis=0).astype(BF)
            m_ref[...] = jnp.full(m_ref.shape, -jnp.inf, F32)
            l_ref[...] = jnp.zeros(l_ref.shape, F32)
            acc_ref[...] = jnp.zeros(acc_ref.shape, F32)

        def partial_softmax(s, vals):
            m = jnp.max(s, axis=-1, keepdims=True)
            pr = jnp.exp2(s - m)
            return m, jnp.sum(pr, axis=-1, keepdims=True), _dot(pr.astype(BF), vals)

        def update(parts):
            m_old = m_ref[...]
            m_new = m_old
            for m, _, _ in parts:
                m_new = jnp.maximum(m_new, m)
            alpha = jnp.exp2(m_old - m_new)
            l_new = alpha * l_ref[...]
            acc = alpha * acc_ref[...]
            for m, l_part, acc_part in parts:
                w = jnp.exp2(m - m_new)
                l_new = l_new + w * l_part
                acc = acc + w * acc_part
            l_ref[...] = l_new
            acc_ref[...] = acc
            m_ref[...] = m_new

        qall = qall_ref[...]
        flat = lambda ref: ref[...].reshape(page_rows, DA_V_DIM).astype(BF)
        per_group = pages // groups
        parts = []
        for g in range(groups):
            sel = slice(g * per_group, (g + 1) * per_group)
            s_g = jnp.concatenate([_dg(qall, flat(kr), _NT) for kr in k_refs[sel]], axis=-1)
            v_g = jnp.concatenate([flat(vr) for vr in v_refs[sel]], axis=0)
            row_head = lax.broadcasted_iota(jnp.int32, s_g.shape, 0) // (2 * s_len)
            key_head = lax.broadcasted_iota(jnp.int32, s_g.shape, 1) % DA_HEADS
            parts.append(partial_softmax(jnp.where(row_head == key_head, s_g, -jnp.inf), v_g))
        update(parts)

        @pl.when(j == pl.num_programs(1) - 1)
        def _():
            kn = kn_ref[0].astype(F32)
            vn = vn_ref[0].astype(F32)
            by_head = lambda x: jnp.concatenate(
                [x[:, h * DA_V_DIM:(h + 1) * DA_V_DIM] for h in range(DA_HEADS)], axis=0).astype(BF)
            s = _dg(qall, by_head(kn), _NT)
            r = lax.broadcasted_iota(jnp.int32, s.shape, 0)
            c = lax.broadcasted_iota(jnp.int32, s.shape, 1)
            keep = (c // s_len == r // (2 * s_len)) & (c % s_len <= r % s_len)
            update([partial_softmax(jnp.where(keep, s, -jnp.inf), by_head(vn))])
            lam = _lambda_full(lq1[...], lk1[...], lq2[...], lk2[...])
            o = acc_ref[...] / l_ref[...]
            heads = []
            for h in range(DA_HEADS):
                o1 = o[(2 * h) * s_len:(2 * h + 1) * s_len]
                o2 = o[(2 * h + 1) * s_len:(2 * h + 2) * s_len]
                heads.append(_subln(o1 - lam * o2, sg_ref[...]))
            o_ref[0] = jnp.concatenate(heads, axis=-1).astype(BF)

    return kernel


def _sample_attention(q, kb, vb, cache_k, cache_v, page_table, p, pages):
    bsz, s_len, _ = q.shape
    n_pages = page_table.shape[1]
    rows = DA_HEADS * 2 * s_len
    tok = pl.BlockSpec((1, s_len, DA_WIDTH), lambda b, j, pt: (b, 0, 0))
    lam_spec = pl.BlockSpec((1, DA_HEAD_DIM), lambda b, j, pt: (0, 0))

    def page_spec(i):
        return pl.BlockSpec((None, None, PAGE_SIZE, DA_HEADS, DA_V_DIM),
                            lambda b, j, pt: (0, pt[b * n_pages + j * pages + i], 0, 0, 0))

    grid_spec = pltpu.PrefetchScalarGridSpec(
        num_scalar_prefetch=1,
        grid=(bsz, n_pages // pages),
        in_specs=[tok, tok, tok, lam_spec, lam_spec, lam_spec, lam_spec,
                  pl.BlockSpec((1, DA_V_DIM), lambda b, j, pt: (0, 0))]
                 + [page_spec(i) for i in range(pages)] * 2,
        out_specs=tok,
        scratch_shapes=[pltpu.VMEM((rows, DA_V_DIM), BF),
                        pltpu.VMEM((rows, 1), F32),
                        pltpu.VMEM((rows, 1), F32),
                        pltpu.VMEM((rows, DA_V_DIM), F32)])
    return pl.pallas_call(
        _make_sample_attn_kernel(pages),
        grid_spec=grid_spec,
        out_shape=jax.ShapeDtypeStruct((bsz, s_len, DA_WIDTH), BF),
        compiler_params=_cp(("parallel", "arbitrary")),
        name="sample_attn",
    )(page_table.reshape(-1), q, kb, vb,
      p["lambda_q1"][None, :], p["lambda_k1"][None, :], p["lambda_q2"][None, :],
      p["lambda_k2"][None, :], p["subln_g"][None, :], *([cache_k] * pages), *([cache_v] * pages))


def _merge_kernel(ya_ref, ob_ref, zg_ref, x_ref, wa_ref, wb_ref, wo_ref, g2_ref,
                  wr_hi_ref, wr_lo_ref, br_ref, h_ref, hn_ref, te_ref, tg_ref):
    zg = zg_ref[...]
    m = (_sigmoid(zg[:, :D_MODEL]) * _dot(ya_ref[...], wa_ref[...])
         + _sigmoid(zg[:, D_MODEL:]) * _dot(ob_ref[...], wb_ref[...]))
    h = x_ref[...] + _dot(m.astype(BF), wo_ref[...])
    h_ref[...] = h
    hn = h * lax.rsqrt(jnp.mean(h * h, axis=-1, keepdims=True) + NORM_EPS) * g2_ref[...]
    tm = hn.shape[0]
    for j in range(ROW_TILES):
        hn_ref[pl.ds(j, tm, stride=ROW_TILES), :] = hn[:, j * 128:(j + 1) * 128]

    hh, hl = _split(hn)
    logits = (_dg(wr_hi_ref[...], hh, _NT) + _dg(wr_hi_ref[...], hl, _NT)
              + _dg(wr_lo_ref[...], hh, _NT)) + br_ref[...]
    eidx = lax.broadcasted_iota(jnp.int32, logits.shape, 0)
    vals, idxs = [], []
    for _ in range(TOP_K):
        mx = jnp.max(logits, axis=0, keepdims=True)
        am = jnp.min(jnp.where(logits == mx, eidx, N_EXPERTS), axis=0, keepdims=True)
        vals.append(mx)
        idxs.append(am)
        logits = jnp.where(eidx == am, -jnp.inf, logits)
    ex = [jnp.exp(vv - vals[0]) for vv in vals]
    den = ex[0] + ex[1] + ex[2] + ex[3]
    te_ref[...] = jnp.concatenate(idxs, axis=0)
    tg_ref[...] = jnp.concatenate([e / den for e in ex], axis=0)


def _merge(ya, ob, zg, x2, p, tm):
    n = x2.shape[0]
    wr_t = p["w_router"].T
    wr_hi = wr_t.astype(BF)
    wr_lo = (wr_t - wr_hi.astype(F32)).astype(BF)
    row = lambda w: pl.BlockSpec((tm, w), lambda i: (i, 0))
    mat = lambda r, c: pl.BlockSpec((r, c), lambda i: (0, 0))
    colblk = pl.BlockSpec((TOP_K, tm), lambda i: (0, i))
    return pl.pallas_call(
        _merge_kernel,
        grid=(n // tm,),
        in_specs=[row(RW_WIDTH), row(DA_WIDTH), row(2 * D_MODEL), row(D_MODEL),
                  mat(RW_WIDTH, D_MODEL), mat(DA_WIDTH, D_MODEL), mat(D_MODEL, D_MODEL),
                  mat(1, D_MODEL), mat(N_EXPERTS, D_MODEL), mat(N_EXPERTS, D_MODEL),
                  mat(N_EXPERTS, 1)],
        out_specs=[row(D_MODEL), pl.BlockSpec((tm * ROW_TILES, 128), lambda i: (i, 0)),
                   colblk, colblk],
        out_shape=[jax.ShapeDtypeStruct((n, D_MODEL), F32),
                   jax.ShapeDtypeStruct((n * ROW_TILES, 128), F32),
                   jax.ShapeDtypeStruct((TOP_K, n), jnp.int32),
                   jax.ShapeDtypeStruct((TOP_K, n), F32)],
        compiler_params=_cp(("parallel",)),
        name="merge_router",
    )(ya, ob, zg, x2, p["w_a"].astype(BF), p["w_b"].astype(BF), p["w_o"].astype(BF),
      p["norm2_g"][None, :], wr_hi, wr_lo, p["b_router"][:, None])


def _experts_kernel(ib_ref, ie_ref, ilo_ref, ihi_ref, ifirst_ref, ilast_ref, ifused_ref,
                    idx_hbm, x_hbm, wu_ref, bu_ref, wd_ref, bd_ref, out_hbm,
                    idx_smem, idx_sem, xbuf, gsem, xb_ref, acc_ref, obuf, ssem):
    i = pl.program_id(0)
    bm = MOE_ROWS
    slab = bm * ROW_TILES
    n_blocks = idx_hbm.shape[0]
    blk = ib_ref[i]
    s2 = blk % 2

    def idx_copy(b):
        slot = b % IDX_SLOTS
        return pltpu.make_async_copy(idx_hbm.at[b], idx_smem.at[pl.ds(slot * 2 * bm, 2 * bm)],
                                     idx_sem.at[slot])

    def tile_rows(row):
        return pl.ds(pl.multiple_of(row * ROW_TILES, ROW_TILES), ROW_TILES)

    def gather_row(b, r):
        tok = idx_smem[(b % IDX_SLOTS) * 2 * bm + r]
        return pltpu.make_async_copy(x_hbm.at[tile_rows(tok)], xbuf.at[tile_rows((b % 2) * bm + r)],
                                     gsem.at[b % 2])

    def scatter_row(b, r):
        dst = idx_smem[(b % IDX_SLOTS) * 2 * bm + bm + r]
        return pltpu.make_async_copy(obuf.at[tile_rows((b % 2) * bm + r)], out_hbm.at[tile_rows(dst)],
                                     ssem.at[b % 2])

    def issue(copy, b, unrolled):
        if unrolled:
            for r in range(bm):
                copy(b, r).start()
        else:
            def body(r, carry):
                copy(b, r).start()
                return carry
            lax.fori_loop(0, bm, body, 0, unroll=8)

    def block_rows(slot2):
        return pl.ds(pl.multiple_of(slot2 * slab, slab), slab)

    def wait_gather(slot2):
        pltpu.make_async_copy(x_hbm.at[pl.ds(0, slab)], xbuf.at[block_rows(slot2)], gsem.at[slot2]).wait()

    def wait_scatter(slot2):
        pltpu.make_async_copy(obuf.at[block_rows(slot2)], out_hbm.at[pl.ds(0, slab)], ssem.at[slot2]).wait()

    def load_tokens():
        base = s2 * slab
        x = jnp.concatenate([xbuf[pl.ds(base + j, bm, stride=ROW_TILES), :] for j in range(ROW_TILES)],
                            axis=-1)
        return x.astype(BF)

    def mlp(x):
        h = _dot(x, wu_ref[...]) + bu_ref[...]
        hg = jnp.minimum(h[:, :D_FF], SWIGLU_LIMIT)
        hl = jnp.clip(h[:, D_FF:], -SWIGLU_LIMIT, SWIGLU_LIMIT)
        act = hg * _sigmoid(SWIGLU_ALPHA * hg) * (hl + 1.0)
        return _dot(act.astype(BF), wd_ref[...]) + bd_ref[...]

    def store_rows(y):
        base = s2 * slab
        for j in range(ROW_TILES):
            obuf[pl.ds(base + j, bm, stride=ROW_TILES), :] = y[:, j * 128:(j + 1) * 128]

    first = ifirst_ref[i] == 1
    fused = ifused_ref[i] == 1

    @pl.when(i == 0)
    def _():
        idx_copy(0).start()
        idx_copy(0).wait()
        issue(gather_row, 0, False)
        if n_blocks > 1:
            idx_copy(1).start()

    @pl.when(first & (blk + 2 < n_blocks))
    def _():
        idx_copy(blk + 2).start()

    @pl.when(fused)
    def _():
        wait_gather(s2)
        x = load_tokens()
        issue(scatter_row, blk - 1, True)
        idx_copy(blk + 1).wait()
        issue(gather_row, blk + 1, True)
        y = mlp(x)
        wait_scatter(s2)
        store_rows(y)

    @pl.when(jnp.logical_not(fused))
    def _():
        @pl.when(first)
        def _():
            wait_gather(s2)

            @pl.when(blk >= 1)
            def _():
                issue(scatter_row, blk - 1, False)

            @pl.when(blk + 1 < n_blocks)
            def _():
                idx_copy(blk + 1).wait()
                issue(gather_row, blk + 1, False)

            xb_ref[...] = load_tokens()
            acc_ref[...] = jnp.zeros(acc_ref.shape, F32)

        lo = ilo_ref[i]
        hi = ihi_ref[i]

        @pl.when(hi > lo)
        def _():
            y = mlp(xb_ref[...])
            rowi = lax.broadcasted_iota(jnp.int32, (bm, 1), 0)
            acc_ref[...] += jnp.where((rowi >= lo) & (rowi < hi), y, 0.0)

        @pl.when(ilast_ref[i] == 1)
        def _():
            @pl.when(blk >= 2)
            def _():
                wait_scatter(s2)

            store_rows(acc_ref[...])

    @pl.when(i == pl.num_programs(0) - 1)
    def _():
        issue(scatter_row, n_blocks - 1, False)
        wait_scatter((n_blocks - 1) % 2)
        if n_blocks > 1:
            wait_scatter((n_blocks - 2) % 2)


def _experts(hn_rows, items, idx, w_up, b_up, w_down, b_down):
    bm = MOE_ROWS
    n_blocks = idx.shape[0]
    n_items = items[0].shape[0]
    wspec = lambda r, c: pl.BlockSpec((None, r, c), lambda i, ib, ie, *_: (ie[i], 0, 0))
    grid_spec = pltpu.PrefetchScalarGridSpec(
        num_scalar_prefetch=7,
        grid=(n_items,),
        in_specs=[pl.BlockSpec(memory_space=pl.ANY), pl.BlockSpec(memory_space=pl.ANY),
                  wspec(D_MODEL, 2 * D_FF), wspec(1, 2 * D_FF), wspec(D_FF, D_MODEL), wspec(1, D_MODEL)],
        out_specs=pl.BlockSpec(memory_space=pl.ANY),
        scratch_shapes=[pltpu.SMEM((IDX_SLOTS * 2 * bm,), jnp.int32),
                        pltpu.SemaphoreType.DMA((IDX_SLOTS,)),
                        pltpu.VMEM((2 * bm * ROW_TILES, 128), F32),
                        pltpu.SemaphoreType.DMA((2,)),
                        pltpu.VMEM((bm, D_MODEL), BF),
                        pltpu.VMEM((bm, D_MODEL), F32),
                        pltpu.VMEM((2 * bm * ROW_TILES, 128), F32),
                        pltpu.SemaphoreType.DMA((2,))])
    return pl.pallas_call(
        _experts_kernel,
        grid_spec=grid_spec,
        out_shape=jax.ShapeDtypeStruct((n_blocks * bm * ROW_TILES, 128), F32),
        compiler_params=_cp(("arbitrary",)),
        name="experts",
    )(*items, idx, hn_rows, w_up, b_up[:, None, :], w_down, b_down[:, None, :])


def _combine_kernel(h_ref, g_ref, *rest):
    e_refs = rest[:TOP_K]
    o_ref = rest[TOP_K]
    tm = h_ref.shape[0]
    g = g_ref[...]
    for j in range(ROW_TILES):
        acc = h_ref[:, j * 128:(j + 1) * 128]
        for kk in range(TOP_K):
            acc = acc + g[:, kk:kk + 1] * e_refs[kk][pl.ds(j, tm, stride=ROW_TILES), :]
        o_ref[:, j * 128:(j + 1) * 128] = acc


def _combine(h, gates_t, expert_rows, tm):
    n = h.shape[0]
    nt = n // tm
    slot = lambda kk: pl.BlockSpec((tm * ROW_TILES, 128), lambda i: (kk * nt + i, 0))
    return pl.pallas_call(
        _combine_kernel,
        grid=(nt,),
        in_specs=[pl.BlockSpec((tm, D_MODEL), lambda i: (i, 0)),
                  pl.BlockSpec((tm, TOP_K), lambda i: (i, 0))]
                 + [slot(kk) for kk in range(TOP_K)],
        out_specs=pl.BlockSpec((tm, D_MODEL), lambda i: (i, 0)),
        out_shape=jax.ShapeDtypeStruct((n, D_MODEL), F32),
        compiler_params=_cp(("parallel",)),
        name="combine",
    )(h, gates_t, *([expert_rows] * TOP_K))


def _route(top_e):
    n = top_e.shape[1]
    nk = n * TOP_K
    bm = MOE_ROWS
    n_blocks = nk // bm
    n_items = n_blocks + N_EXPERTS
    flat_e = top_e.reshape(-1)
    _, sorted_slot = lax.sort((flat_e, jnp.arange(nk, dtype=jnp.int32)), num_keys=1)
    experts = jnp.arange(N_EXPERTS, dtype=jnp.int32)
    counts = jnp.sum((flat_e[None, :] == experts[:, None]).astype(jnp.int32), axis=1)
    end = jnp.cumsum(counts)
    start = end - counts
    first_blk = start // bm
    n_e = jnp.where(counts > 0, (end - 1) // bm - first_blk + 1, 0)
    cum = jnp.cumsum(n_e)
    off = cum - n_e
    total = cum[-1]
    it = jnp.arange(n_items, dtype=jnp.int32)
    valid = it < total
    e_of = lambda t: jnp.minimum(jnp.sum((cum[None, :] <= t[:, None]).astype(jnp.int32), axis=1),
                                 N_EXPERTS - 1)
    e_i = jnp.where(valid, e_of(it), e_of(total[None] - 1)[0])
    blk = jnp.where(valid, first_blk[e_i] + it - off[e_i], n_blocks - 1)
    lo = jnp.where(valid, jnp.clip(start[e_i] - blk * bm, 0, bm), 0)
    hi = jnp.where(valid, jnp.clip(end[e_i] - blk * bm, 0, bm), 0)
    prev_blk = jnp.concatenate([jnp.full((1,), -1, jnp.int32), blk[:-1]])
    next_blk = jnp.concatenate([blk[1:], jnp.full((1,), -1, jnp.int32)])
    first = valid & (blk != prev_blk)
    last = valid & ((blk != next_blk) | (it == total - 1))
    fused = first & last & (blk >= 2) & (blk + 1 < n_blocks)
    items = tuple(x.astype(jnp.int32) for x in (blk, e_i, lo, hi, first, last, fused))
    idx = jnp.concatenate([(sorted_slot % n).reshape(n_blocks, bm), sorted_slot.reshape(n_blocks, bm)],
                          axis=1)
    return items, idx


def _moe(h, hn_rows, top_e, top_g, wts, tm):
    items, idx = _route(top_e)
    expert_rows = _experts(hn_rows, items, idx, *wts)
    return _combine(h, top_g.T, expert_rows, tm)


def _pick(n, pref):
    t = min(n, pref)
    assert n % t == 0
    return t


def _layer(x, pos, s0, shift0, p, w_in_bf, moe_wts, attend, chunk):
    b, t, _ = x.shape
    n = b * t
    tm = _pick(n, 256)
    tt = _pick(t, 256)
    x2 = x.reshape(n, D_MODEL)
    zr, zq, zk, zv, zvb, zg = _inproj(x2, p["norm1_g"][None, :], w_in_bf, tm)
    zr3 = zr.reshape(b, t, RW_IN)
    q, k, kb = _qk_prep(zq.reshape(b, t, DA_WIDTH), zk.reshape(b, t, DA_WIDTH),
                        p["q_norm_g"], p["k_norm_g"], pos, tt)
    r, lw, km, v, a, bb, g = _rwkv_prep(zr3, shift0, p, tt)
    ya, s_t = _rwkv_chunk(r, lw, km, v, a, bb, g, s0, p, chunk, _pick(t // chunk, 4))
    ob = attend(q, kb, zvb.reshape(b, t, DA_WIDTH))
    h, hn, top_e, top_g = _merge(ya.reshape(n, RW_WIDTH), ob.reshape(n, DA_WIDTH), zg, x2, p, tm)
    y = _moe(h, hn, top_e, top_g, moe_wts, tm)
    return (y.reshape(b, t, D_MODEL), k.reshape(b, t, DA_HEADS, 2 * DA_HEAD_DIM),
            zv.reshape(b, t, DA_HEADS, DA_V_DIM), s_t, zr3[:, -1])


def kernel(x_prompt, x_sample, cache_k, cache_v, page_table, state_wkv, state_shift, norm1_g, w_in, mu_shift, w0, w_w2, a0, a_a2, g_g2, k_k, k_a, r_k, ln_x_w, ln_x_b, q_norm_g, k_norm_g, lambda_q1, lambda_k1, lambda_q2, lambda_k2, subln_g, w_a, w_b, w_o, norm2_g, w_router, b_router, w_up, b_up, w_down, b_down):
    names = ["norm1_g", "mu_shift", "w0", "w_w2", "a0", "a_a2", "g_g2", "k_k", "k_a", "r_k",
             "ln_x_w", "ln_x_b", "q_norm_g", "k_norm_g", "lambda_q1", "lambda_k1", "lambda_q2",
             "lambda_k2", "subln_g", "w_a", "w_b", "w_o", "norm2_g", "w_router", "b_router"]
    vals = [norm1_g, mu_shift, w0, w_w2, a0, a_a2, g_g2, k_k, k_a, r_k, ln_x_w, ln_x_b,
            q_norm_g, k_norm_g, lambda_q1, lambda_k1, lambda_q2, lambda_k2, subln_g,
            w_a, w_b, w_o, norm2_g, w_router, b_router]
    p = {nm: vv[0] for nm, vv in zip(names, vals)}
    w_in_bf = w_in[0].astype(BF)
    moe_wts = (w_up[0].astype(BF), b_up[0], w_down[0].astype(BF), b_down[0])

    bp, tp, _ = x_prompt.shape
    bs, ts, _ = x_sample.shape
    past = page_table.shape[1] * PAGE_SIZE

    attend_p = lambda q, kb, vb: _prompt_attention(q, kb, vb, p, _pick(tp, 1024))
    yp, kp, vp, wp, sp = _layer(
        x_prompt, jnp.arange(tp), jnp.zeros((bp, RW_HEADS, RW_HEAD_DIM, RW_HEAD_DIM), F32),
        jnp.zeros((bp, RW_IN), F32), p, w_in_bf, moe_wts, attend_p, _pick(tp, 64))

    attend_s = lambda q, kb, vb: _sample_attention(
        q, kb, vb, cache_k, cache_v, page_table, p, _pick(page_table.shape[1], 8))
    ys, ks, vs, ws, ss = _layer(
        x_sample, past + jnp.arange(ts), state_wkv[0], state_shift[0], p, w_in_bf, moe_wts,
        attend_s, ts)

    return (yp, ys, kp[None], vp[None], wp[None], sp[None],
            ks[None], vs[None], ws[None], ss[None])
```

```python
import functools
import math

import jax
import jax.numpy as jnp
import numpy as np
from jax import lax
from jax.experimental import pallas as pl
from jax.experimental.pallas import tpu as pltpu

F32 = jnp.float32
BF = jnp.bfloat16

D_MODEL = 1024
PAGE_SIZE = 128
RW_HEADS = 8
RW_HEAD_DIM = 64
RW_WIDTH = RW_HEADS * RW_HEAD_DIM
DECAY_LORA = 64
AAA_LORA = 64
GATE_LORA = 128
RW_IN = 3 * RW_WIDTH + DECAY_LORA + AAA_LORA + GATE_LORA
GN_EPS = 64e-5
DA_HEADS = 4
DA_HEAD_DIM = 64
DA_V_DIM = 2 * DA_HEAD_DIM
DA_WIDTH = DA_HEADS * DA_V_DIM
ROPE_DIM = DA_HEAD_DIM // 4
ROPE_THETA = 500000.0
N_IN = RW_IN + 3 * DA_WIDTH + 2 * D_MODEL
N_EXPERTS = 32
TOP_K = 4
D_FF = D_MODEL
SWIGLU_ALPHA = 1.702
SWIGLU_LIMIT = 7.0
NORM_EPS = 1e-5
LOG2_E = 1.4426950408889634
LAMBDA_INIT = 0.8 - 0.6 * math.exp(-0.3 * 0)

VMEM_LIMIT = 56 * 1024 * 1024
MOE_ROWS = 256
IDX_SLOTS = 4
ROW_TILES = D_MODEL // 128
assert ROW_TILES == 8

_NT = (((1,), (1,)), ((), ()))
_TN = (((0,), (0,)), ((), ()))


def _cp(sem, vmem=VMEM_LIMIT):
    return pltpu.CompilerParams(dimension_semantics=sem, vmem_limit_bytes=vmem)


def _dot(a, b):
    return jnp.dot(a, b, preferred_element_type=F32)


def _dg(a, b, dims):
    return lax.dot_general(a, b, dims, preferred_element_type=F32)


def _split(x):
    hi = x.astype(BF)
    lo = (x - hi.astype(F32)).astype(BF)
    return hi, lo


def _dot_exact_rhs(a, b_bf16):
    ah, al = _split(a)
    return _dot(ah, b_bf16) + _dot(al, b_bf16)


def _sigmoid(x):
    return 1.0 / (1.0 + jnp.exp(-x))


_SEGS = (0, RW_IN, RW_IN + DA_WIDTH, RW_IN + 2 * DA_WIDTH, RW_IN + 3 * DA_WIDTH, N_IN)


def _store_by_head(o_ref, x):
    rows = x.shape[0]
    for h in range(DA_HEADS):
        o_ref[pl.ds(h, rows, stride=DA_HEADS), :] = x[:, h * DA_V_DIM:(h + 1) * DA_V_DIM]


def _inproj_kernel(x_ref, g_ref, w_ref, zr_ref, zq_ref, zk_ref, zv_ref, zvb_ref, zg_ref):
    x = x_ref[...]
    xn = x * lax.rsqrt(jnp.mean(x * x, axis=-1, keepdims=True) + NORM_EPS) * g_ref[...]
    xb = xn.astype(BF)
    zr_ref[...] = _dot(xb, w_ref[:, _SEGS[0]:_SEGS[1]])
    zq_ref[...] = _dot(xb, w_ref[:, _SEGS[1]:_SEGS[2]])
    zk_ref[...] = _dot(xb, w_ref[:, _SEGS[2]:_SEGS[3]])
    zv = _dot(xb, w_ref[:, _SEGS[3]:_SEGS[4]])
    _store_by_head(zv_ref, zv)
    zvb_ref[...] = zv.astype(BF)
    zg_ref[...] = _dot(xb, w_ref[:, _SEGS[4]:_SEGS[5]])


def _inproj(x2, g, w_bf, tm):
    n = x2.shape[0]
    row = lambda w: pl.BlockSpec((tm, w), lambda i: (i, 0))
    return pl.pallas_call(
        _inproj_kernel,
        grid=(n // tm,),
        in_specs=[row(D_MODEL),
                  pl.BlockSpec((1, D_MODEL), lambda i: (0, 0)),
                  pl.BlockSpec((D_MODEL, N_IN), lambda i: (0, 0))],
        out_specs=[row(RW_IN), row(DA_WIDTH), row(DA_WIDTH),
                   pl.BlockSpec((tm * DA_HEADS, DA_V_DIM), lambda i: (i, 0)), row(DA_WIDTH),
                   row(2 * D_MODEL)],
        out_shape=[jax.ShapeDtypeStruct((n, RW_IN), F32),
                   jax.ShapeDtypeStruct((n, DA_WIDTH), F32),
                   jax.ShapeDtypeStruct((n, DA_WIDTH), F32),
                   jax.ShapeDtypeStruct((n * DA_HEADS, DA_V_DIM), F32),
                   jax.ShapeDtypeStruct((n, DA_WIDTH), BF),
                   jax.ShapeDtypeStruct((n, 2 * D_MODEL), F32)],
        compiler_params=_cp(("parallel",)),
        name="inproj",
    )(x2, g, w_bf)


def _group_ones(width, group):
    i = np.arange(width)
    return jnp.asarray((i[:, None] // group) == (i[None, :] // group), dtype=BF)


def _qk_prep_kernel(zq_ref, zk_ref, gq_ref, gk_ref, cos_ref, sa_ref, sb_ref, j_ref,
                    q_ref, k_ref, kb_ref):
    cos = jnp.concatenate([cos_ref[...]] * (DA_WIDTH // 128), axis=-1)
    sa = jnp.concatenate([sa_ref[...]] * (DA_WIDTH // 128), axis=-1)
    sb = jnp.concatenate([sb_ref[...]] * (DA_WIDTH // 128), axis=-1)

    def norm_rope(z, g):
        ms = _dot_exact_rhs(z * z, j_ref[...]) * (1.0 / DA_HEAD_DIM)
        y = z * lax.rsqrt(ms + NORM_EPS) * g
        half = ROPE_DIM // 2
        up = pltpu.roll(y, DA_WIDTH - half, axis=1)
        dn = pltpu.roll(y, half, axis=1)
        return y * cos + up * sa + dn * sb

    q = norm_rope(zq_ref[0], gq_ref[...])
    k = norm_rope(zk_ref[0], gk_ref[...])
    q_ref[0] = (q * (DA_HEAD_DIM ** -0.5 * LOG2_E)).astype(BF)
    _store_by_head(k_ref, k)
    kb_ref[0] = k.astype(BF)


def _rope_tables(pos):
    half = ROPE_DIM // 2
    inv = ROPE_THETA ** (-jnp.arange(half, dtype=F32) / half)
    ang = pos.astype(F32)[:, None] * inv[None, :]
    lane = np.arange(128) % DA_HEAD_DIM
    idx = lane % half
    c = jnp.cos(ang)[:, idx]
    s = jnp.sin(ang)[:, idx]
    in_rope = jnp.asarray(lane < ROPE_DIM)
    first = jnp.asarray(lane < half)
    second = jnp.asarray((lane >= half) & (lane < ROPE_DIM))
    cos_t = jnp.where(in_rope[None, :], c, 1.0)
    sa_t = jnp.where(first[None, :], -s, 0.0)
    sb_t = jnp.where(second[None, :], s, 0.0)
    return cos_t, sa_t, sb_t


def _qk_prep(zq, zk, gq, gk, pos, tt):
    b, t, _ = zq.shape
    cos_t, sa_t, sb_t = _rope_tables(pos)
    gq_t = jnp.tile(gq, DA_WIDTH // DA_HEAD_DIM)[None, :]
    gk_t = jnp.tile(gk, DA_WIDTH // DA_HEAD_DIM)[None, :]
    tok = pl.BlockSpec((1, tt, DA_WIDTH), lambda i, j: (i, j, 0))
    tab = pl.BlockSpec((tt, 128), lambda i, j: (j, 0))
    par = pl.BlockSpec((1, DA_WIDTH), lambda i, j: (0, 0))
    return pl.pallas_call(
        _qk_prep_kernel,
        grid=(b, t // tt),
        in_specs=[tok, tok, par, par, tab, tab, tab,
                  pl.BlockSpec((DA_WIDTH, DA_WIDTH), lambda i, j: (0, 0))],
        out_specs=[tok, pl.BlockSpec((tt * DA_HEADS, DA_V_DIM), lambda i, j: (i * (t // tt) + j, 0)), tok],
        out_shape=[jax.ShapeDtypeStruct((b, t, DA_WIDTH), BF),
                   jax.ShapeDtypeStruct((b * t * DA_HEADS, DA_V_DIM), F32),
                   jax.ShapeDtypeStruct((b, t, DA_WIDTH), BF)],
        compiler_params=_cp(("parallel", "parallel")),
        name="qk_prep",
    )(zq, zk, gq_t, gk_t, cos_t, sa_t, sb_t, _group_ones(DA_WIDTH, DA_HEAD_DIM))


def _rwkv_prep_kernel(zr_ref, sh_ref, mu_ref, w0_ref, a0_ref, kk_ref, ka_ref,
                      w2_ref, a2_ref, g2_ref, j_ref,
                      r_ref, lw_ref, k_ref, v_ref, a_ref, b_ref, g_ref, carry_ref):
    tt = zr_ref.shape[1]

    @pl.when(pl.program_id(1) == 0)
    def _():
        carry_ref[...] = sh_ref[0]

    z = zr_ref[0]
    rolled = pltpu.roll(z, 1, axis=0)
    first_row = lax.broadcasted_iota(jnp.int32, (tt, 1), 0) == 0
    zprev = jnp.where(first_row, carry_ref[...], rolled)
    carry_ref[...] = z[tt - 1:tt, :]
    zs = z + (zprev - z) * mu_ref[...]

    r = zs[:, 0:RW_WIDTH]
    k = zs[:, RW_WIDTH:2 * RW_WIDTH]
    v = zs[:, 2 * RW_WIDTH:3 * RW_WIDTH]
    wa = zs[:, 3 * RW_WIDTH:3 * RW_WIDTH + DECAY_LORA + AAA_LORA]
    gd = zs[:, 3 * RW_WIDTH + DECAY_LORA + AAA_LORA:]

    lw = w0_ref[...] + _dot(jnp.tanh(wa).astype(BF), w2_ref[...])
    log_decay = -math.exp(-0.5) * _sigmoid(lw)
    a = _sigmoid(a0_ref[...] + _dot(wa.astype(BF), a2_ref[...]))
    g = _dot(_sigmoid(gd).astype(BF), g2_ref[...])
    kk = k * kk_ref[...]
    ss = _dot_exact_rhs(kk * kk, j_ref[...])
    kk = kk / jnp.maximum(jnp.sqrt(ss), 1e-12)
    kmod = k * (1.0 + (a - 1.0) * ka_ref[...])

    r_ref[0] = r
    lw_ref[0] = log_decay
    k_ref[0] = kmod
    v_ref[0] = v
    a_ref[0] = -kk
    b_ref[0] = kk * a
    g_ref[0] = g


def _rwkv_prep(zr, shift0, p, tt):
    b, t, _ = zr.shape
    zero_pad = jnp.zeros((DECAY_LORA, RW_WIDTH), F32)
    w2p = jnp.concatenate([p["w_w2"], zero_pad], axis=0).astype(BF)
    a2p = jnp.concatenate([zero_pad, p["a_a2"]], axis=0).astype(BF)
    tok = lambda w: pl.BlockSpec((1, tt, w), lambda i, j: (i, j, 0))
    par = lambda w: pl.BlockSpec((1, w), lambda i, j: (0, 0))
    mat = lambda r, c: pl.BlockSpec((r, c), lambda i, j: (0, 0))
    return pl.pallas_call(
        _rwkv_prep_kernel,
        grid=(b, t // tt),
        in_specs=[tok(RW_IN), pl.BlockSpec((1, 1, RW_IN), lambda i, j: (i, 0, 0)),
                  par(RW_IN), par(RW_WIDTH), par(RW_WIDTH), par(RW_WIDTH), par(RW_WIDTH),
                  mat(DECAY_LORA + AAA_LORA, RW_WIDTH), mat(DECAY_LORA + AAA_LORA, RW_WIDTH),
                  mat(GATE_LORA, RW_WIDTH), mat(RW_WIDTH, RW_WIDTH)],
        out_specs=[tok(RW_WIDTH)] * 7,
        out_shape=[jax.ShapeDtypeStruct((b, t, RW_WIDTH), F32)] * 7,
        scratch_shapes=[pltpu.VMEM((1, RW_IN), F32)],
        compiler_params=_cp(("parallel", "arbitrary")),
        name="rwkv_prep",
    )(zr, shift0[:, None, :], p["mu_shift"][None, :], p["w0"][None, :], p["a0"][None, :],
      p["k_k"][None, :], p["k_a"][None, :], w2p, a2p, p["g_g2"].astype(BF),
      _group_ones(RW_WIDTH, RW_HEAD_DIM))


_NN = (((1,), (0,)), ((), ()))


def _d3(a, b, dims=_NN):
    return _dg(a[0], b[0], dims) + _dg(a[0], b[1], dims) + _dg(a[1], b[0], dims)


def _rwkv_chunk_kernel(r_ref, lw_ref, k_ref, v_ref, a_ref, b_ref, g_ref, s0_ref,
                       lnw_ref, lnb_ref, rk_ref, y_ref, sT_ref, s_ref, *, c):
    n = RW_HEAD_DIM
    n_sub = r_ref.shape[1] // c
    heads = range(RW_HEADS)
    units = [(j, h) for j in range(n_sub) for h in heads]

    @pl.when(pl.program_id(1) == 0)
    def _():
        s_ref[...] = s0_ref[0]

    row = lax.broadcasted_iota(jnp.int32, (c, c), 0)
    col = lax.broadcasted_iota(jnp.int32, (c, c), 1)
    lower = row >= col
    strict = row > col
    eye_c = (row == col).astype(F32)
    tri = _split(lower.astype(F32))

    f32 = {}
    p_chunk = []
    for j in range(n_sub):
        rows = slice(j * c, (j + 1) * c)
        r, lw, k, v, a, b = (ref[0, rows, :] for ref in (r_ref, lw_ref, k_ref, v_ref, a_ref, b_ref))
        cum = _d3(tri, _split(lw))
        cum_last = cum[c - 1:c, :]
        pinv = jnp.exp(-cum)
        pc = jnp.exp(cum_last - cum)
        p_chunk.append(jnp.exp(cum_last))
        full = {"rt": r * jnp.exp(cum), "at": a * jnp.exp(cum - lw), "bt": b * pinv,
                "kt": k * pinv, "bh": b * pc, "kh": k * pc, "v": v, "rk": r * k * rk_ref[...]}
        for nm, x in full.items():
            f32.setdefault(nm, []).extend(x[:, h * n:(h + 1) * n] for h in heads)

    idx = range(len(units))
    bf = lambda xs: [x.astype(BF) for x in xs]
    at, bt, kt, bh, vv = (bf(f32[nm]) for nm in ("at", "bt", "kt", "bh", "v"))
    at_rt = bf([jnp.concatenate([f32["at"][u], f32["rt"][u]], axis=0) for u in idx])

    xb = [_dg(at_rt[u], bt[u], _NT) for u in idx]
    xk = [_dg(at_rt[u], kt[u], _NT) for u in idx]
    a_ab = [jnp.where(strict, x[:c], 0.0) for x in xb]
    a_ak = bf([jnp.where(strict, x[:c], 0.0) for x in xk])
    a_rb = bf([jnp.where(lower, x[c:], 0.0) for x in xb])
    a_rk = bf([jnp.where(lower, x[c:], 0.0) for x in xk])

    t_inv = [eye_c + x for x in a_ab]
    n_sq = int(math.log2(c)) - 1
    if n_sq > 0:
        pw = bf(a_ab)
        pw = [_dot(x, x) for x in pw]
        for it in range(n_sq):
            pw_b = bf(pw)
            if it + 1 < n_sq:
                both = [_dot(jnp.concatenate([t_inv[u].astype(BF), pw_b[u]], axis=0), pw_b[u])
                        for u in idx]
                t_inv = [t_inv[u] + both[u][:c] for u in idx]
                pw = [x[c:] for x in both]
            else:
                t_inv = [t_inv[u] + _dot(t_inv[u].astype(BF), pw_b[u]) for u in idx]
    t_b = bf(t_inv)
    akv = bf([_dot(a_ak[u], vv[u]) for u in idx])
    w1 = bf([_dot(t_b[u], at[u]) for u in idx])
    u0 = bf([_dot(t_b[u], akv[u]) for u in idx])

    krow = lax.broadcasted_iota(jnp.int32, (n, n), 0)
    kcol = lax.broadcasted_iota(jnp.int32, (n, n), 1)
    eye_n = krow == kcol
    m_mat = [_split(jnp.where(eye_n, p_chunk[j][:, h * n:(h + 1) * n], 0.0)
                    + _dg(w1[u], bh[u], _TN)) for u, (j, h) in enumerate(units)]
    n_mat = [_dg(u0[u], bh[u], _TN) + _d3(_split(f32["v"][u]), _split(f32["kh"][u]), _TN) for u in idx]
    g_mat = bf([f32["rt"][u] + _dot(a_rb[u], w1[u]) for u in idx])
    y0 = [_dot(a_rb[u], u0[u]) + _dot(a_rk[u], vv[u]) for u in idx]

    state = [s_ref[h] for h in heads]
    for j in range(n_sub):
        s_old = [_split(state[h]) for h in heads]
        ys = [_dg(g_mat[j * RW_HEADS + h], s_old[h][0], _NT) + y0[j * RW_HEADS + h] for h in heads]
        state = [_d3(s_old[h], m_mat[j * RW_HEADS + h]) + n_mat[j * RW_HEADS + h] for h in heads]
        yn, bonus = [], []
        for h in heads:
            u = j * RW_HEADS + h
            mu = jnp.mean(ys[h], axis=-1, keepdims=True)
            var = jnp.mean(jnp.square(ys[h] - mu), axis=-1, keepdims=True)
            yn.append((ys[h] - mu) * lax.rsqrt(var + GN_EPS))
            bonus.append(jnp.sum(f32["rk"][u], axis=-1, keepdims=True) * f32["v"][u])
        yn = jnp.concatenate(yn, axis=-1)
        bonus = jnp.concatenate(bonus, axis=-1)
        rows = slice(j * c, (j + 1) * c)
        y_ref[0, rows, :] = ((yn * lnw_ref[...] + lnb_ref[...] + bonus) * g_ref[0, rows, :]).astype(BF)
    for h in heads:
        s_ref[h] = state[h]

    @pl.when(pl.program_id(1) == pl.num_programs(1) - 1)
    def _():
        sT_ref[0] = s_ref[...]


def _rwkv_chunk(r, lw, k, v, a, b, g, s0, p, c, n_sub):
    bsz, t, _ = r.shape
    tok = pl.BlockSpec((1, c * n_sub, RW_WIDTH), lambda i, j: (i, j, 0))
    par = pl.BlockSpec((1, RW_WIDTH), lambda i, j: (0, 0))
    st = pl.BlockSpec((1, RW_HEADS, RW_HEAD_DIM, RW_HEAD_DIM), lambda i, j: (i, 0, 0, 0))
    return pl.pallas_call(
        functools.partial(_rwkv_chunk_kernel, c=c),
        grid=(bsz, t // (c * n_sub)),
        in_specs=[tok] * 7 + [st, par, par, par],
        out_specs=[tok, st],
        out_shape=[jax.ShapeDtypeStruct((bsz, t, RW_WIDTH), BF),
                   jax.ShapeDtypeStruct((bsz, RW_HEADS, RW_HEAD_DIM, RW_HEAD_DIM), F32)],
        scratch_shapes=[pltpu.VMEM((RW_HEADS, RW_HEAD_DIM, RW_HEAD_DIM), F32)],
        compiler_params=_cp(("parallel", "arbitrary")),
        name="rwkv_chunk",
    )(r, lw, k, v, a, b, g, s0, p["ln_x_w"][None, :], p["ln_x_b"][None, :],
      p["r_k"].reshape(1, RW_WIDTH))


def _lambda_full(lq1, lk1, lq2, lk2):
    s1 = jnp.sum(lq1 * lk1, axis=-1, keepdims=True)
    s2 = jnp.sum(lq2 * lk2, axis=-1, keepdims=True)
    return jnp.exp(s1) - jnp.exp(s2) + LAMBDA_INIT


def _subln(o, g):
    y = o * lax.rsqrt(jnp.mean(o * o, axis=-1, keepdims=True) + NORM_EPS)
    return y * g * (1.0 - LAMBDA_INIT)


def _prompt_attn_kernel(qi_ref, ki_ref, q_ref, k_ref, v_ref, lq1, lk1, lq2, lk2, sg_ref,
                        o_ref, qm_ref, m_ref, l_ref, acc_ref):
    p = pl.program_id(2)
    qi = qi_ref[p]
    ki = ki_ref[p]
    tq = q_ref.shape[1]
    tk = k_ref.shape[1]

    @pl.when(ki == 0)
    def _():
        q = q_ref[0]
        lane = lax.broadcasted_iota(jnp.int32, q.shape, 1)
        zero = jnp.zeros_like(q)
        qm_ref[0:tq] = jnp.where(lane < DA_HEAD_DIM, q, zero)
        qm_ref[tq:2 * tq] = jnp.where(lane >= DA_HEAD_DIM, q, zero)
        m_ref[...] = jnp.full(m_ref.shape, -jnp.inf, F32)
        l_ref[...] = jnp.zeros(l_ref.shape, F32)
        acc_ref[...] = jnp.zeros(acc_ref.shape, F32)

    def step(masked):
        s = _dg(qm_ref[...], k_ref[0], _NT)
        if masked:
            rowi = lax.broadcasted_iota(jnp.int32, (2 * tq, tk), 0)
            coli = lax.broadcasted_iota(jnp.int32, (2 * tq, tk), 1)
            s = jnp.where(coli <= jnp.where(rowi >= tq, rowi - tq, rowi), s, -jnp.inf)
        m_old = m_ref[...]
        m_new = jnp.maximum(m_old, jnp.max(s, axis=-1, keepdims=True))
        alpha = jnp.exp2(m_old - m_new)
        pr = jnp.exp2(s - jnp.concatenate([m_new] * (tk // 128), axis=-1))
        l_ref[...] = alpha * l_ref[...] + jnp.sum(pr, axis=-1, keepdims=True)
        acc_ref[...] = alpha * acc_ref[...] + _dot(pr.astype(BF), v_ref[0])
        m_ref[...] = m_new

    @pl.when(ki < qi)
    def _():
        step(False)

    @pl.when(ki == qi)
    def _():
        step(True)
        lam = _lambda_full(lq1[...], lk1[...], lq2[...], lk2[...])
        o = acc_ref[...] / l_ref[...]
        o_ref[0] = _subln(o[0:tq] - lam * o[tq:2 * tq], sg_ref[...]).astype(BF)


def _prompt_attention(q, kb, vb, p, tq):
    bsz, t, _ = q.shape
    nq = t // tq
    qi = np.concatenate([np.full(i + 1, i) for i in range(nq)]).astype(np.int32)
    ki = np.concatenate([np.arange(i + 1) for i in range(nq)]).astype(np.int32)
    lam_spec = pl.BlockSpec((1, DA_HEAD_DIM), lambda b, h, s, qt, kt: (0, 0))
    grid_spec = pltpu.PrefetchScalarGridSpec(
        num_scalar_prefetch=2,
        grid=(bsz, DA_HEADS, len(qi)),
        in_specs=[pl.BlockSpec((1, tq, DA_V_DIM), lambda b, h, s, qt, kt: (b, qt[s], h)),
                  pl.BlockSpec((1, tq, DA_V_DIM), lambda b, h, s, qt, kt: (b, kt[s], h)),
                  pl.BlockSpec((1, tq, DA_V_DIM), lambda b, h, s, qt, kt: (b, kt[s], h)),
                  lam_spec, lam_spec, lam_spec, lam_spec,
                  pl.BlockSpec((1, DA_V_DIM), lambda b, h, s, qt, kt: (0, 0))],
        out_specs=pl.BlockSpec((1, tq, DA_V_DIM), lambda b, h, s, qt, kt: (b, qt[s], h)),
        scratch_shapes=[pltpu.VMEM((2 * tq, DA_V_DIM), BF),
                        pltpu.VMEM((2 * tq, 128), F32),
                        pltpu.VMEM((2 * tq, 128), F32),
                        pltpu.VMEM((2 * tq, DA_V_DIM), F32)])
    return pl.pallas_call(
        _prompt_attn_kernel,
        grid_spec=grid_spec,
        out_shape=jax.ShapeDtypeStruct((bsz, t, DA_WIDTH), BF),
        compiler_params=_cp(("parallel", "parallel", "arbitrary")),
        name="prompt_attn",
    )(jnp.asarray(qi), jnp.asarray(ki), q, kb, vb,
      p["lambda_q1"][None, :], p["lambda_k1"][None, :], p["lambda_q2"][None, :],
      p["lambda_k2"][None, :], p["subln_g"][None, :])


def _make_sample_attn_kernel(pages):
    n_pairs = DA_HEADS * 2
    groups = 1

    def kernel(pt_ref, q_ref, kn_ref, vn_ref, lq1, lk1, lq2, lk2, sg_ref, *rest):
        k_refs = rest[:pages]
        v_refs = rest[pages:2 * pages]
        o_ref = rest[2 * pages]
        qall_ref, m_ref, l_ref, acc_ref = rest[2 * pages + 1:]
        j = pl.program_id(1)
        s_len = q_ref.shape[1]
        rows = n_pairs * s_len
        page_rows = PAGE_SIZE * DA_HEADS

        @pl.when(j == 0)
        def _():
            q = q_ref[0].astype(F32)
            lane = lax.broadcasted_iota(jnp.int32, (s_len, DA_V_DIM), 1)
            pieces = []
            for h in range(DA_HEADS):
                qh = q[:, h * DA_V_DIM:(h + 1) * DA_V_DIM]
                pieces.append(jnp.where(lane < DA_HEAD_DIM, qh, 0.0))
                pieces.append(jnp.where(lane >= DA_HEAD_DIM, qh, 0.0))
            qall_ref[...] = jnp.concatenate(pieces, axis=0).astype(BF)
            m_ref[...] = jnp.full(m_ref.shape, -jnp.inf, F32)
            l_ref[...] = jnp.zeros(l_ref.shape, F32)
            acc_ref[...] = jnp.zeros(acc_ref.shape, F32)

        def partial_softmax(s, vals):
            m = jnp.max(s, axis=-1, keepdims=True)
            pr = jnp.exp2(s - m)
            return m, jnp.sum(pr, axis=-1, keepdims=True), _dot(pr.astype(BF), vals)

        def update(parts):
            m_old = m_ref[...]
            m_new = m_old
            for m, _, _ in parts:
                m_new = jnp.maximum(m_new, m)
            alpha = jnp.exp2(m_old - m_new)
            l_new = alpha * l_ref[...]
            acc = alpha * acc_ref[...]
            for m, l_part, acc_part in parts:
                w = jnp.exp2(m - m_new)
                l_new = l_new + w * l_part
                acc = acc + w * acc_part
            l_ref[...] = l_new
            acc_ref[...] = acc
            m_ref[...] = m_new

        qall = qall_ref[...]
        flat = lambda ref: ref[...].reshape(page_rows, DA_V_DIM).astype(BF)
        per_group = pages // groups
        parts = []
        for g in range(groups):
            sel = slice(g * per_group, (g + 1) * per_group)
            s_g = jnp.concatenate([_dg(qall, flat(kr), _NT) for kr in k_refs[sel]], axis=-1)
            v_g = jnp.concatenate([flat(vr) for vr in v_refs[sel]], axis=0)
            row_head = lax.broadcasted_iota(jnp.int32, s_g.shape, 0) // (2 * s_len)
            key_head = lax.broadcasted_iota(jnp.int32, s_g.shape, 1) % DA_HEADS
            parts.append(partial_softmax(jnp.where(row_head == key_head, s_g, -jnp.inf), v_g))
        update(parts)

        @pl.when(j == pl.num_programs(1) - 1)
        def _():
            kn = kn_ref[0].astype(F32)
            vn = vn_ref[0].astype(F32)
            by_head = lambda x: jnp.concatenate(
                [x[:, h * DA_V_DIM:(h + 1) * DA_V_DIM] for h in range(DA_HEADS)], axis=0).astype(BF)
            s = _dg(qall, by_head(kn), _NT)
            r = lax.broadcasted_iota(jnp.int32, s.shape, 0)
            c = lax.broadcasted_iota(jnp.int32, s.shape, 1)
            keep = (c // s_len == r // (2 * s_len)) & (c % s_len <= r % s_len)
            update([partial_softmax(jnp.where(keep, s, -jnp.inf), by_head(vn))])
            lam = _lambda_full(lq1[...], lk1[...], lq2[...], lk2[...])
            o = acc_ref[...] / l_ref[...]
            heads = []
            for h in range(DA_HEADS):
                o1 = o[(2 * h) * s_len:(2 * h + 1) * s_len]
                o2 = o[(2 * h + 1) * s_len:(2 * h + 2) * s_len]
                heads.append(_subln(o1 - lam * o2, sg_ref[...]))
            o_ref[0] = jnp.concatenate(heads, axis=-1).astype(BF)

    return kernel


def _sample_attention(q, kb, vb, cache_k, cache_v, page_table, p, pages):
    bsz, s_len, _ = q.shape
    n_pages = page_table.shape[1]
    rows = DA_HEADS * 2 * s_len
    tok = pl.BlockSpec((1, s_len, DA_WIDTH), lambda b, j, pt: (b, 0, 0))
    lam_spec = pl.BlockSpec((1, DA_HEAD_DIM), lambda b, j, pt: (0, 0))

    def page_spec(i):
        return pl.BlockSpec((None, None, PAGE_SIZE, DA_HEADS, DA_V_DIM),
                            lambda b, j, pt: (0, pt[b * n_pages + j * pages + i], 0, 0, 0))

    grid_spec = pltpu.PrefetchScalarGridSpec(
        num_scalar_prefetch=1,
        grid=(bsz, n_pages // pages),
        in_specs=[tok, tok, tok, lam_spec, lam_spec, lam_spec, lam_spec,
                  pl.BlockSpec((1, DA_V_DIM), lambda b, j, pt: (0, 0))]
                 + [page_spec(i) for i in range(pages)] * 2,
        out_specs=tok,
        scratch_shapes=[pltpu.VMEM((rows, DA_V_DIM), BF),
                        pltpu.VMEM((rows, 1), F32),
                        pltpu.VMEM((rows, 1), F32),
                        pltpu.VMEM((rows, DA_V_DIM), F32)])
    return pl.pallas_call(
        _make_sample_attn_kernel(pages),
        grid_spec=grid_spec,
        out_shape=jax.ShapeDtypeStruct((bsz, s_len, DA_WIDTH), BF),
        compiler_params=_cp(("parallel", "arbitrary")),
        name="sample_attn",
    )(page_table.reshape(-1), q, kb, vb,
      p["lambda_q1"][None, :], p["lambda_k1"][None, :], p["lambda_q2"][None, :],
      p["lambda_k2"][None, :], p["subln_g"][None, :], *([cache_k] * pages), *([cache_v] * pages))


def _merge_kernel(ya_ref, ob_ref, zg_ref, x_ref, wa_ref, wb_ref, wo_ref, g2_ref,
                  wr_hi_ref, wr_lo_ref, br_ref, h_ref, hn_ref, te_ref, tg_ref):
    zg = zg_ref[...]
    m = (_sigmoid(zg[:, :D_MODEL]) * _dot(ya_ref[...], wa_ref[...])
         + _sigmoid(zg[:, D_MODEL:]) * _dot(ob_ref[...], wb_ref[...]))
    h = x_ref[...] + _dot(m.astype(BF), wo_ref[...])
    h_ref[...] = h
    hn = h * lax.rsqrt(jnp.mean(h * h, axis=-1, keepdims=True) + NORM_EPS) * g2_ref[...]
    tm = hn.shape[0]
    for j in range(ROW_TILES):
        hn_ref[pl.ds(j, tm, stride=ROW_TILES), :] = hn[:, j * 128:(j + 1) * 128]

    hh, hl = _split(hn)
    logits = (_dg(wr_hi_ref[...], hh, _NT) + _dg(wr_hi_ref[...], hl, _NT)
              + _dg(wr_lo_ref[...], hh, _NT)) + br_ref[...]
    eidx = lax.broadcasted_iota(jnp.int32, logits.shape, 0)
    vals, idxs = [], []
    for _ in range(TOP_K):
        mx = jnp.max(logits, axis=0, keepdims=True)
        am = jnp.min(jnp.where(logits == mx, eidx, N_EXPERTS), axis=0, keepdims=True)
        vals.append(mx)
        idxs.append(am)
        logits = jnp.where(eidx == am, -jnp.inf, logits)
    ex = [jnp.exp(vv - vals[0]) for vv in vals]
    den = ex[0] + ex[1] + ex[2] + ex[3]
    te_ref[...] = jnp.concatenate(idxs, axis=0)
    tg_ref[...] = jnp.concatenate([e / den for e in ex], axis=0)


def _merge(ya, ob, zg, x2, p, tm):
    n = x2.shape[0]
    wr_t = p["w_router"].T
    wr_hi = wr_t.astype(BF)
    wr_lo = (wr_t - wr_hi.astype(F32)).astype(BF)
    row = lambda w: pl.BlockSpec((tm, w), lambda i: (i, 0))
    mat = lambda r, c: pl.BlockSpec((r, c), lambda i: (0, 0))
    colblk = pl.BlockSpec((TOP_K, tm), lambda i: (0, i))
    return pl.pallas_call(
        _merge_kernel,
        grid=(n // tm,),
        in_specs=[row(RW_WIDTH), row(DA_WIDTH), row(2 * D_MODEL), row(D_MODEL),
                  mat(RW_WIDTH, D_MODEL), mat(DA_WIDTH, D_MODEL), mat(D_MODEL, D_MODEL),
                  mat(1, D_MODEL), mat(N_EXPERTS, D_MODEL), mat(N_EXPERTS, D_MODEL),
                  mat(N_EXPERTS, 1)],
        out_specs=[row(D_MODEL), pl.BlockSpec((tm * ROW_TILES, 128), lambda i: (i, 0)),
                   colblk, colblk],
        out_shape=[jax.ShapeDtypeStruct((n, D_MODEL), F32),
                   jax.ShapeDtypeStruct((n * ROW_TILES, 128), F32),
                   jax.ShapeDtypeStruct((TOP_K, n), jnp.int32),
                   jax.ShapeDtypeStruct((TOP_K, n), F32)],
        compiler_params=_cp(("parallel",)),
        name="merge_router",
    )(ya, ob, zg, x2, p["w_a"].astype(BF), p["w_b"].astype(BF), p["w_o"].astype(BF),
      p["norm2_g"][None, :], wr_hi, wr_lo, p["b_router"][:, None])


def _experts_kernel(ib_ref, ie_ref, ilo_ref, ihi_ref, ifirst_ref, ilast_ref, ifused_ref,
                    idx_hbm, x_hbm, wu_ref, bu_ref, wd_ref, bd_ref, out_hbm,
                    idx_smem, idx_sem, xbuf, gsem, xb_ref, acc_ref, obuf, ssem):
    i = pl.program_id(0)
    bm = MOE_ROWS
    slab = bm * ROW_TILES
    n_blocks = idx_hbm.shape[0]
    blk = ib_ref[i]
    s2 = blk % 2

    def idx_copy(b):
        slot = b % IDX_SLOTS
        return pltpu.make_async_copy(idx_hbm.at[b], idx_smem.at[pl.ds(slot * 2 * bm, 2 * bm)],
                                     idx_sem.at[slot])

    def tile_rows(row):
        return pl.ds(pl.multiple_of(row * ROW_TILES, ROW_TILES), ROW_TILES)

    def gather_row(b, r):
        tok = idx_smem[(b % IDX_SLOTS) * 2 * bm + r]
        return pltpu.make_async_copy(x_hbm.at[tile_rows(tok)], xbuf.at[tile_rows((b % 2) * bm + r)],
                                     gsem.at[b % 2])

    def scatter_row(b, r):
        dst = idx_smem[(b % IDX_SLOTS) * 2 * bm + bm + r]
        return pltpu.make_async_copy(obuf.at[tile_rows((b % 2) * bm + r)], out_hbm.at[tile_rows(dst)],
                                     ssem.at[b % 2])

    def issue(copy, b, unrolled):
        if unrolled:
            for r in range(bm):
                copy(b, r).start()
        else:
            def body(r, carry):
                copy(b, r).start()
                return carry
            lax.fori_loop(0, bm, body, 0, unroll=8)

    def block_rows(slot2):
        return pl.ds(pl.multiple_of(slot2 * slab, slab), slab)

    def wait_gather(slot2):
        pltpu.make_async_copy(x_hbm.at[pl.ds(0, slab)], xbuf.at[block_rows(slot2)], gsem.at[slot2]).wait()

    def wait_scatter(slot2):
        pltpu.make_async_copy(obuf.at[block_rows(slot2)], out_hbm.at[pl.ds(0, slab)], ssem.at[slot2]).wait()

    def load_tokens():
        base = s2 * slab
        x = jnp.concatenate([xbuf[pl.ds(base + j, bm, stride=ROW_TILES), :] for j in range(ROW_TILES)],
                            axis=-1)
        return x.astype(BF)

    def mlp(x):
        h = _dot(x, wu_ref[...]) + bu_ref[...]
        hg = jnp.minimum(h[:, :D_FF], SWIGLU_LIMIT)
        hl = jnp.clip(h[:, D_FF:], -SWIGLU_LIMIT, SWIGLU_LIMIT)
        act = hg * _sigmoid(SWIGLU_ALPHA * hg) * (hl + 1.0)
        return _dot(act.astype(BF), wd_ref[...]) + bd_ref[...]

    def store_rows(y):
        base = s2 * slab
        for j in range(ROW_TILES):
            obuf[pl.ds(base + j, bm, stride=ROW_TILES), :] = y[:, j * 128:(j + 1) * 128]

    first = ifirst_ref[i] == 1
    fused = ifused_ref[i] == 1

    @pl.when(i == 0)
    def _():
        idx_copy(0).start()
        idx_copy(0).wait()
        issue(gather_row, 0, False)
        if n_blocks > 1:
            idx_copy(1).start()

    @pl.when(first & (blk + 2 < n_blocks))
    def _():
        idx_copy(blk + 2).start()

    @pl.when(fused)
    def _():
        wait_gather(s2)
        x = load_tokens()
        issue(scatter_row, blk - 1, True)
        idx_copy(blk + 1).wait()
        issue(gather_row, blk + 1, True)
        y = mlp(x)
        wait_scatter(s2)
        store_rows(y)

    @pl.when(jnp.logical_not(fused))
    def _():
        @pl.when(first)
        def _():
            wait_gather(s2)

            @pl.when(blk >= 1)
            def _():
                issue(scatter_row, blk - 1, False)

            @pl.when(blk + 1 < n_blocks)
            def _():
                idx_copy(blk + 1).wait()
                issue(gather_row, blk + 1, False)

            xb_ref[...] = load_tokens()
            acc_ref[...] = jnp.zeros(acc_ref.shape, F32)

        lo = ilo_ref[i]
        hi = ihi_ref[i]

        @pl.when(hi > lo)
        def _():
            y = mlp(xb_ref[...])
            rowi = lax.broadcasted_iota(jnp.int32, (bm, 1), 0)
            acc_ref[...] += jnp.where((rowi >= lo) & (rowi < hi), y, 0.0)

        @pl.when(ilast_ref[i] == 1)
        def _():
            @pl.when(blk >= 2)
            def _():
                wait_scatter(s2)

            store_rows(acc_ref[...])

    @pl.when(i == pl.num_programs(0) - 1)
    def _():
        issue(scatter_row, n_blocks - 1, False)
        wait_scatter((n_blocks - 1) % 2)
        if n_blocks > 1:
            wait_scatter((n_blocks - 2) % 2)


def _experts(hn_rows, items, idx, w_up, b_up, w_down, b_down):
    bm = MOE_ROWS
    n_blocks = idx.shape[0]
    n_items = items[0].shape[0]
    wspec = lambda r, c: pl.BlockSpec((None, r, c), lambda i, ib, ie, *_: (ie[i], 0, 0))
    grid_spec = pltpu.PrefetchScalarGridSpec(
        num_scalar_prefetch=7,
        grid=(n_items,),
        in_specs=[pl.BlockSpec(memory_space=pl.ANY), pl.BlockSpec(memory_space=pl.ANY),
                  wspec(D_MODEL, 2 * D_FF), wspec(1, 2 * D_FF), wspec(D_FF, D_MODEL), wspec(1, D_MODEL)],
        out_specs=pl.BlockSpec(memory_space=pl.ANY),
        scratch_shapes=[pltpu.SMEM((IDX_SLOTS * 2 * bm,), jnp.int32),
                        pltpu.SemaphoreType.DMA((IDX_SLOTS,)),
                        pltpu.VMEM((2 * bm * ROW_TILES, 128), F32),
                        pltpu.SemaphoreType.DMA((2,)),
                        pltpu.VMEM((bm, D_MODEL), BF),
                        pltpu.VMEM((bm, D_MODEL), F32),
                        pltpu.VMEM((2 * bm * ROW_TILES, 128), F32),
                        pltpu.SemaphoreType.DMA((2,))])
    return pl.pallas_call(
        _experts_kernel,
        grid_spec=grid_spec,
        out_shape=jax.ShapeDtypeStruct((n_blocks * bm * ROW_TILES, 128), F32),
        compiler_params=_cp(("arbitrary",)),
        name="experts",
    )(*items, idx, hn_rows, w_up, b_up[:, None, :], w_down, b_down[:, None, :])


def _combine_kernel(h_ref, g_ref, *rest):
    e_refs = rest[:TOP_K]
    o_ref = rest[TOP_K]
    tm = h_ref.shape[0]
    g = g_ref[...]
    for j in range(ROW_TILES):
        acc = h_ref[:, j * 128:(j + 1) * 128]
        for kk in range(TOP_K):
            acc = acc + g[:, kk:kk + 1] * e_refs[kk][pl.ds(j, tm, stride=ROW_TILES), :]
        o_ref[:, j * 128:(j + 1) * 128] = acc


def _combine(h, gates_t, expert_rows, tm):
    n = h.shape[0]
    nt = n // tm
    slot = lambda kk: pl.BlockSpec((tm * ROW_TILES, 128), lambda i: (kk * nt + i, 0))
    return pl.pallas_call(
        _combine_kernel,
        grid=(nt,),
        in_specs=[pl.BlockSpec((tm, D_MODEL), lambda i: (i, 0)),
                  pl.BlockSpec((tm, TOP_K), lambda i: (i, 0))]
                 + [slot(kk) for kk in range(TOP_K)],
        out_specs=pl.BlockSpec((tm, D_MODEL), lambda i: (i, 0)),
        out_shape=jax.ShapeDtypeStruct((n, D_MODEL), F32),
        compiler_params=_cp(("parallel",)),
        name="combine",
    )(h, gates_t, *([expert_rows] * TOP_K))


def _route(top_e):
    n = top_e.shape[1]
    nk = n * TOP_K
    bm = MOE_ROWS
    n_blocks = nk // bm
    n_items = n_blocks + N_EXPERTS
    flat_e = top_e.reshape(-1)
    _, sorted_slot = lax.sort((flat_e, jnp.arange(nk, dtype=jnp.int32)), num_keys=1)
    experts = jnp.arange(N_EXPERTS, dtype=jnp.int32)
    counts = jnp.sum((flat_e[None, :] == experts[:, None]).astype(jnp.int32), axis=1)
    end = jnp.cumsum(counts)
    start = end - counts
    first_blk = start // bm
    n_e = jnp.where(counts > 0, (end - 1) // bm - first_blk + 1, 0)
    cum = jnp.cumsum(n_e)
    off = cum - n_e
    total = cum[-1]
    it = jnp.arange(n_items, dtype=jnp.int32)
    valid = it < total
    e_of = lambda t: jnp.minimum(jnp.sum((cum[None, :] <= t[:, None]).astype(jnp.int32), axis=1),
                                 N_EXPERTS - 1)
    e_i = jnp.where(valid, e_of(it), e_of(total[None] - 1)[0])
    blk = jnp.where(valid, first_blk[e_i] + it - off[e_i], n_blocks - 1)
    lo = jnp.where(valid, jnp.clip(start[e_i] - blk * bm, 0, bm), 0)
    hi = jnp.where(valid, jnp.clip(end[e_i] - blk * bm, 0, bm), 0)
    prev_blk = jnp.concatenate([jnp.full((1,), -1, jnp.int32), blk[:-1]])
    next_blk = jnp.concatenate([blk[1:], jnp.full((1,), -1, jnp.int32)])
    first = valid & (blk != prev_blk)
    last = valid & ((blk != next_blk) | (it == total - 1))
    fused = first & last & (blk >= 2) & (blk + 1 < n_blocks)
    items = tuple(x.astype(jnp.int32) for x in (blk, e_i, lo, hi, first, last, fused))
    idx = jnp.concatenate([(sorted_slot % n).reshape(n_blocks, bm), sorted_slot.reshape(n_blocks, bm)],
                          axis=1)
    return items, idx


def _moe(h, hn_rows, top_e, top_g, wts, tm):
    items, idx = _route(top_e)
    expert_rows = _experts(hn_rows, items, idx, *wts)
    return _combine(h, top_g.T, expert_rows, tm)


def _pick(n, pref):
    t = min(n, pref)
    assert n % t == 0
    return t


def _layer(x, pos, s0, shift0, p, w_in_bf, moe_wts, attend, chunk):
    b, t, _ = x.shape
    n = b * t
    tm = _pick(n, 256)
    tb = _pick(n, 512)
    tt = _pick(t, 512)
    x2 = x.reshape(n, D_MODEL)
    zr, zq, zk, zv, zvb, zg = _inproj(x2, p["norm1_g"][None, :], w_in_bf, tm)
    zr3 = zr.reshape(b, t, RW_IN)
    q, k, kb = _qk_prep(zq.reshape(b, t, DA_WIDTH), zk.reshape(b, t, DA_WIDTH),
                        p["q_norm_g"], p["k_norm_g"], pos, tt)
    r, lw, km, v, a, bb, g = _rwkv_prep(zr3, shift0, p, tt)
    ya, s_t = _rwkv_chunk(r, lw, km, v, a, bb, g, s0, p, chunk, _pick(t // chunk, 4))
    ob = attend(q, kb, zvb.reshape(b, t, DA_WIDTH))
    h, hn, top_e, top_g = _merge(ya.reshape(n, RW_WIDTH), ob.reshape(n, DA_WIDTH), zg, x2, p, tb)
    y = _moe(h, hn, top_e, top_g, moe_wts, tb)
    return (y.reshape(b, t, D_MODEL), k.reshape(b, t, DA_HEADS, 2 * DA_HEAD_DIM),
            zv.reshape(b, t, DA_HEADS, DA_V_DIM), s_t, zr3[:, -1])


def kernel(x_prompt, x_sample, cache_k, cache_v, page_table, state_wkv, state_shift, norm1_g, w_in, mu_shift, w0, w_w2, a0, a_a2, g_g2, k_k, k_a, r_k, ln_x_w, ln_x_b, q_norm_g, k_norm_g, lambda_q1, lambda_k1, lambda_q2, lambda_k2, subln_g, w_a, w_b, w_o, norm2_g, w_router, b_router, w_up, b_up, w_down, b_down):
    names = ["norm1_g", "mu_shift", "w0", "w_w2", "a0", "a_a2", "g_g2", "k_k", "k_a", "r_k",
             "ln_x_w", "ln_x_b", "q_norm_g", "k_norm_g", "lambda_q1", "lambda_k1", "lambda_q2",
             "lambda_k2", "subln_g", "w_a", "w_b", "w_o", "norm2_g", "w_router", "b_router"]
    vals = [norm1_g, mu_shift, w0, w_w2, a0, a_a2, g_g2, k_k, k_a, r_k, ln_x_w, ln_x_b,
            q_norm_g, k_norm_g, lambda_q1, lambda_k1, lambda_q2, lambda_k2, subln_g,
            w_a, w_b, w_o, norm2_g, w_router, b_router]
    p = {nm: vv[0] for nm, vv in zip(names, vals)}
    w_in_bf = w_in[0].astype(BF)
    moe_wts = (w_up[0].astype(BF), b_up[0], w_down[0].astype(BF), b_down[0])

    bp, tp, _ = x_prompt.shape
    bs, ts, _ = x_sample.shape
    past = page_table.shape[1] * PAGE_SIZE

    attend_p = lambda q, kb, vb: _prompt_attention(q, kb, vb, p, _pick(tp, 1024))
    yp, kp, vp, wp, sp = _layer(
        x_prompt, jnp.arange(tp), jnp.zeros((bp, RW_HEADS, RW_HEAD_DIM, RW_HEAD_DIM), F32),
        jnp.zeros((bp, RW_IN), F32), p, w_in_bf, moe_wts, attend_p, _pick(tp, 64))

    attend_s = lambda q, kb, vb: _sample_attention(
        q, kb, vb, cache_k, cache_v, page_table, p, _pick(page_table.shape[1], 8))
    ys, ks, vs, ws, ss = _layer(
        x_sample, past + jnp.arange(ts), state_wkv[0], state_shift[0], p, w_in_bf, moe_wts,
        attend_s, ts)

    return (yp, ys, kp[None], vp[None], wp[None], sp[None],
            ks[None], vs[None], ws[None], ss[None])
```

```python
import functools
import math

import jax
import jax.numpy as jnp
import numpy as np
from jax import lax
from jax.experimental import pallas as pl
from jax.experimental.pallas import tpu as pltpu

F32 = jnp.float32
BF = jnp.bfloat16

D_MODEL = 1024
PAGE_SIZE = 128
RW_HEADS = 8
RW_HEAD_DIM = 64
RW_WIDTH = RW_HEADS * RW_HEAD_DIM
DECAY_LORA = 64
AAA_LORA = 64
GATE_LORA = 128
RW_IN = 3 * RW_WIDTH + DECAY_LORA + AAA_LORA + GATE_LORA
GN_EPS = 64e-5
DA_HEADS = 4
DA_HEAD_DIM = 64
DA_V_DIM = 2 * DA_HEAD_DIM
DA_WIDTH = DA_HEADS * DA_V_DIM
ROPE_DIM = DA_HEAD_DIM // 4
ROPE_THETA = 500000.0
N_IN = RW_IN + 3 * DA_WIDTH + 2 * D_MODEL
N_EXPERTS = 32
TOP_K = 4
D_FF = D_MODEL
SWIGLU_ALPHA = 1.702
SWIGLU_LIMIT = 7.0
NORM_EPS = 1e-5
LOG2_E = 1.4426950408889634
LAMBDA_INIT = 0.8 - 0.6 * math.exp(-0.3 * 0)

VMEM_LIMIT = 56 * 1024 * 1024
MOE_ROWS = 256
IDX_SLOTS = 4
ROW_TILES = D_MODEL // 128
assert ROW_TILES == 8

_NT = (((1,), (1,)), ((), ()))
_TN = (((0,), (0,)), ((), ()))


def _cp(sem, vmem=VMEM_LIMIT):
    return pltpu.CompilerParams(dimension_semantics=sem, vmem_limit_bytes=vmem)


def _dot(a, b):
    return jnp.dot(a, b, preferred_element_type=F32)


def _dg(a, b, dims):
    return lax.dot_general(a, b, dims, preferred_element_type=F32)


def _split(x):
    hi = x.astype(BF)
    lo = (x - hi.astype(F32)).astype(BF)
    return hi, lo


def _dot_exact_rhs(a, b_bf16):
    ah, al = _split(a)
    return _dot(ah, b_bf16) + _dot(al, b_bf16)


def _sigmoid(x):
    return 1.0 / (1.0 + jnp.exp(-x))


_SEGS = (0, RW_IN, RW_IN + DA_WIDTH, RW_IN + 2 * DA_WIDTH, RW_IN + 3 * DA_WIDTH, N_IN)


def _store_by_head(o_ref, x):
    rows = x.shape[0]
    for h in range(DA_HEADS):
        o_ref[pl.ds(h, rows, stride=DA_HEADS), :] = x[:, h * DA_V_DIM:(h + 1) * DA_V_DIM]


def _inproj_kernel(x_ref, g_ref, w_ref, zr_ref, zq_ref, zk_ref, zv_ref, zvb_ref, zg_ref):
    x = x_ref[...]
    xn = x * lax.rsqrt(jnp.mean(x * x, axis=-1, keepdims=True) + NORM_EPS) * g_ref[...]
    xb = xn.astype(BF)
    zr_ref[...] = _dot(xb, w_ref[:, _SEGS[0]:_SEGS[1]])
    zq_ref[...] = _dot(xb, w_ref[:, _SEGS[1]:_SEGS[2]])
    zk_ref[...] = _dot(xb, w_ref[:, _SEGS[2]:_SEGS[3]])
    zv = _dot(xb, w_ref[:, _SEGS[3]:_SEGS[4]])
    _store_by_head(zv_ref, zv)
    zvb_ref[...] = zv.astype(BF)
    zg_ref[...] = _dot(xb, w_ref[:, _SEGS[4]:_SEGS[5]])


def _inproj(x2, g, w_bf, tm):
    n = x2.shape[0]
    row = lambda w: pl.BlockSpec((tm, w), lambda i: (i, 0))
    return pl.pallas_call(
        _inproj_kernel,
        grid=(n // tm,),
        in_specs=[row(D_MODEL),
                  pl.BlockSpec((1, D_MODEL), lambda i: (0, 0)),
                  pl.BlockSpec((D_MODEL, N_IN), lambda i: (0, 0))],
        out_specs=[row(RW_IN), row(DA_WIDTH), row(DA_WIDTH),
                   pl.BlockSpec((tm * DA_HEADS, DA_V_DIM), lambda i: (i, 0)), row(DA_WIDTH),
                   row(2 * D_MODEL)],
        out_shape=[jax.ShapeDtypeStruct((n, RW_IN), F32),
                   jax.ShapeDtypeStruct((n, DA_WIDTH), F32),
                   jax.ShapeDtypeStruct((n, DA_WIDTH), F32),
                   jax.ShapeDtypeStruct((n * DA_HEADS, DA_V_DIM), F32),
                   jax.ShapeDtypeStruct((n, DA_WIDTH), BF),
                   jax.ShapeDtypeStruct((n, 2 * D_MODEL), F32)],
        compiler_params=_cp(("parallel",)),
        name="inproj",
    )(x2, g, w_bf)


def _group_ones(width, group):
    i = np.arange(width)
    return jnp.asarray((i[:, None] // group) == (i[None, :] // group), dtype=BF)


def _qk_prep_kernel(zq_ref, zk_ref, gq_ref, gk_ref, cos_ref, sa_ref, sb_ref, j_ref,
                    q_ref, k_ref, kb_ref):
    cos = jnp.concatenate([cos_ref[...]] * (DA_WIDTH // 128), axis=-1)
    sa = jnp.concatenate([sa_ref[...]] * (DA_WIDTH // 128), axis=-1)
    sb = jnp.concatenate([sb_ref[...]] * (DA_WIDTH // 128), axis=-1)

    def norm_rope(z, g):
        ms = _dot_exact_rhs(z * z, j_ref[...]) * (1.0 / DA_HEAD_DIM)
        y = z * lax.rsqrt(ms + NORM_EPS) * g
        half = ROPE_DIM // 2
        up = pltpu.roll(y, DA_WIDTH - half, axis=1)
        dn = pltpu.roll(y, half, axis=1)
        return y * cos + up * sa + dn * sb

    q = norm_rope(zq_ref[0], gq_ref[...])
    k = norm_rope(zk_ref[0], gk_ref[...])
    q_ref[0] = (q * (DA_HEAD_DIM ** -0.5 * LOG2_E)).astype(BF)
    _store_by_head(k_ref, k)
    kb_ref[0] = k.astype(BF)


def _rope_tables(pos):
    half = ROPE_DIM // 2
    inv = ROPE_THETA ** (-jnp.arange(half, dtype=F32) / half)
    ang = pos.astype(F32)[:, None] * inv[None, :]
    lane = np.arange(128) % DA_HEAD_DIM
    idx = lane % half
    c = jnp.cos(ang)[:, idx]
    s = jnp.sin(ang)[:, idx]
    in_rope = jnp.asarray(lane < ROPE_DIM)
    first = jnp.asarray(lane < half)
    second = jnp.asarray((lane >= half) & (lane < ROPE_DIM))
    cos_t = jnp.where(in_rope[None, :], c, 1.0)
    sa_t = jnp.where(first[None, :], -s, 0.0)
    sb_t = jnp.where(second[None, :], s, 0.0)
    return cos_t, sa_t, sb_t


def _qk_prep(zq, zk, gq, gk, pos, tt):
    b, t, _ = zq.shape
    cos_t, sa_t, sb_t = _rope_tables(pos)
    gq_t = jnp.tile(gq, DA_WIDTH // DA_HEAD_DIM)[None, :]
    gk_t = jnp.tile(gk, DA_WIDTH // DA_HEAD_DIM)[None, :]
    tok = pl.BlockSpec((1, tt, DA_WIDTH), lambda i, j: (i, j, 0))
    tab = pl.BlockSpec((tt, 128), lambda i, j: (j, 0))
    par = pl.BlockSpec((1, DA_WIDTH), lambda i, j: (0, 0))
    return pl.pallas_call(
        _qk_prep_kernel,
        grid=(b, t // tt),
        in_specs=[tok, tok, par, par, tab, tab, tab,
                  pl.BlockSpec((DA_WIDTH, DA_WIDTH), lambda i, j: (0, 0))],
        out_specs=[tok, pl.BlockSpec((tt * DA_HEADS, DA_V_DIM), lambda i, j: (i * (t // tt) + j, 0)), tok],
        out_shape=[jax.ShapeDtypeStruct((b, t, DA_WIDTH), BF),
                   jax.ShapeDtypeStruct((b * t * DA_HEADS, DA_V_DIM), F32),
                   jax.ShapeDtypeStruct((b, t, DA_WIDTH), BF)],
        compiler_params=_cp(("parallel", "parallel")),
        name="qk_prep",
    )(zq, zk, gq_t, gk_t, cos_t, sa_t, sb_t, _group_ones(DA_WIDTH, DA_HEAD_DIM))


def _rwkv_prep_kernel(zr_ref, sh_ref, mu_ref, w0_ref, a0_ref, kk_ref, ka_ref,
                      w2_ref, a2_ref, g2_ref, j_ref,
                      r_ref, lw_ref, k_ref, v_ref, a_ref, b_ref, g_ref, carry_ref):
    tt = zr_ref.shape[1]

    @pl.when(pl.program_id(1) == 0)
    def _():
        carry_ref[...] = sh_ref[0]

    z = zr_ref[0]
    rolled = pltpu.roll(z, 1, axis=0)
    first_row = lax.broadcasted_iota(jnp.int32, (tt, 1), 0) == 0
    zprev = jnp.where(first_row, carry_ref[...], rolled)
    carry_ref[...] = z[tt - 1:tt, :]
    zs = z + (zprev - z) * mu_ref[...]

    r = zs[:, 0:RW_WIDTH]
    k = zs[:, RW_WIDTH:2 * RW_WIDTH]
    v = zs[:, 2 * RW_WIDTH:3 * RW_WIDTH]
    wa = zs[:, 3 * RW_WIDTH:3 * RW_WIDTH + DECAY_LORA + AAA_LORA]
    gd = zs[:, 3 * RW_WIDTH + DECAY_LORA + AAA_LORA:]

    lw = w0_ref[...] + _dot(jnp.tanh(wa).astype(BF), w2_ref[...])
    log_decay = -math.exp(-0.5) * _sigmoid(lw)
    a = _sigmoid(a0_ref[...] + _dot(wa.astype(BF), a2_ref[...]))
    g = _dot(_sigmoid(gd).astype(BF), g2_ref[...])
    kk = k * kk_ref[...]
    ss = _dot_exact_rhs(kk * kk, j_ref[...])
    kk = kk / jnp.maximum(jnp.sqrt(ss), 1e-12)
    kmod = k * (1.0 + (a - 1.0) * ka_ref[...])

    r_ref[0] = r
    lw_ref[0] = log_decay
    k_ref[0] = kmod
    v_ref[0] = v
    a_ref[0] = -kk
    b_ref[0] = kk * a
    g_ref[0] = g


def _rwkv_prep(zr, shift0, p, tt):
    b, t, _ = zr.shape
    zero_pad = jnp.zeros((DECAY_LORA, RW_WIDTH), F32)
    w2p = jnp.concatenate([p["w_w2"], zero_pad], axis=0).astype(BF)
    a2p = jnp.concatenate([zero_pad, p["a_a2"]], axis=0).astype(BF)
    tok = lambda w: pl.BlockSpec((1, tt, w), lambda i, j: (i, j, 0))
    par = lambda w: pl.BlockSpec((1, w), lambda i, j: (0, 0))
    mat = lambda r, c: pl.BlockSpec((r, c), lambda i, j: (0, 0))
    return pl.pallas_call(
        _rwkv_prep_kernel,
        grid=(b, t // tt),
        in_specs=[tok(RW_IN), pl.BlockSpec((1, 1, RW_IN), lambda i, j: (i, 0, 0)),
                  par(RW_IN), par(RW_WIDTH), par(RW_WIDTH), par(RW_WIDTH), par(RW_WIDTH),
                  mat(DECAY_LORA + AAA_LORA, RW_WIDTH), mat(DECAY_LORA + AAA_LORA, RW_WIDTH),
                  mat(GATE_LORA, RW_WIDTH), mat(RW_WIDTH, RW_WIDTH)],
        out_specs=[tok(RW_WIDTH)] * 7,
        out_shape=[jax.ShapeDtypeStruct((b, t, RW_WIDTH), F32)] * 7,
        scratch_shapes=[pltpu.VMEM((1, RW_IN), F32)],
        compiler_params=_cp(("parallel", "arbitrary")),
        name="rwkv_prep",
    )(zr, shift0[:, None, :], p["mu_shift"][None, :], p["w0"][None, :], p["a0"][None, :],
      p["k_k"][None, :], p["k_a"][None, :], w2p, a2p, p["g_g2"].astype(BF),
      _group_ones(RW_WIDTH, RW_HEAD_DIM))


_NN = (((1,), (0,)), ((), ()))


def _d3(a, b, dims=_NN):
    return _dg(a[0], b[0], dims) + _dg(a[0], b[1], dims) + _dg(a[1], b[0], dims)


def _rwkv_chunk_kernel(r_ref, lw_ref, k_ref, v_ref, a_ref, b_ref, g_ref, s0_ref,
                       lnw_ref, lnb_ref, rk_ref, y_ref, sT_ref, s_ref, *, c):
    n = RW_HEAD_DIM
    n_sub = r_ref.shape[1] // c
    heads = range(RW_HEADS)
    units = [(j, h) for j in range(n_sub) for h in heads]

    @pl.when(pl.program_id(1) == 0)
    def _():
        s_ref[...] = s0_ref[0]

    row = lax.broadcasted_iota(jnp.int32, (c, c), 0)
    col = lax.broadcasted_iota(jnp.int32, (c, c), 1)
    lower = row >= col
    strict = row > col
    eye_c = (row == col).astype(F32)
    tri = _split(lower.astype(F32))

    f32 = {}
    p_chunk = []
    for j in range(n_sub):
        rows = slice(j * c, (j + 1) * c)
        r, lw, k, v, a, b = (ref[0, rows, :] for ref in (r_ref, lw_ref, k_ref, v_ref, a_ref, b_ref))
        cum = _d3(tri, _split(lw))
        cum_last = cum[c - 1:c, :]
        pinv = jnp.exp(-cum)
        pc = jnp.exp(cum_last - cum)
        p_chunk.append(jnp.exp(cum_last))
        full = {"rt": r * jnp.exp(cum), "at": a * jnp.exp(cum - lw), "bt": b * pinv,
                "kt": k * pinv, "bh": b * pc, "kh": k * pc, "v": v, "rk": r * k * rk_ref[...]}
        for nm, x in full.items():
            f32.setdefault(nm, []).extend(x[:, h * n:(h + 1) * n] for h in heads)

    idx = range(len(units))
    bf = lambda xs: [x.astype(BF) for x in xs]
    at, bt, kt, bh, vv = (bf(f32[nm]) for nm in ("at", "bt", "kt", "bh", "v"))
    at_rt = bf([jnp.concatenate([f32["at"][u], f32["rt"][u]], axis=0) for u in idx])

    xb = [_dg(at_rt[u], bt[u], _NT) for u in idx]
    xk = [_dg(at_rt[u], kt[u], _NT) for u in idx]
    a_ab = [jnp.where(strict, x[:c], 0.0) for x in xb]
    a_ak = bf([jnp.where(strict, x[:c], 0.0) for x in xk])
    a_rb = bf([jnp.where(lower, x[c:], 0.0) for x in xb])
    a_rk = bf([jnp.where(lower, x[c:], 0.0) for x in xk])

    t_inv = [eye_c + x for x in a_ab]
    n_sq = int(math.log2(c)) - 1
    if n_sq > 0:
        pw = bf(a_ab)
        pw = [_dot(x, x) for x in pw]
        for it in range(n_sq):
            pw_b = bf(pw)
            if it + 1 < n_sq:
                both = [_dot(jnp.concatenate([t_inv[u].astype(BF), pw_b[u]], axis=0), pw_b[u])
                        for u in idx]
                t_inv = [t_inv[u] + both[u][:c] for u in idx]
                pw = [x[c:] for x in both]
            else:
                t_inv = [t_inv[u] + _dot(t_inv[u].astype(BF), pw_b[u]) for u in idx]
    t_b = bf(t_inv)
    akv = bf([_dot(a_ak[u], vv[u]) for u in idx])
    w1 = bf([_dot(t_b[u], at[u]) for u in idx])
    u0 = bf([_dot(t_b[u], akv[u]) for u in idx])

    krow = lax.broadcasted_iota(jnp.int32, (n, n), 0)
    kcol = lax.broadcasted_iota(jnp.int32, (n, n), 1)
    eye_n = krow == kcol
    m_mat = [_split(jnp.where(eye_n, p_chunk[j][:, h * n:(h + 1) * n], 0.0)
                    + _dg(w1[u], bh[u], _TN)) for u, (j, h) in enumerate(units)]
    n_mat = [_dg(u0[u], bh[u], _TN) + _d3(_split(f32["v"][u]), _split(f32["kh"][u]), _TN) for u in idx]
    g_mat = bf([f32["rt"][u] + _dot(a_rb[u], w1[u]) for u in idx])
    y0 = [_dot(a_rb[u], u0[u]) + _dot(a_rk[u], vv[u]) for u in idx]

    state = [s_ref[h] for h in heads]
    for j in range(n_sub):
        s_old = [_split(state[h]) for h in heads]
        ys = [_dg(g_mat[j * RW_HEADS + h], s_old[h][0], _NT) + y0[j * RW_HEADS + h] for h in heads]
        state = [_d3(s_old[h], m_mat[j * RW_HEADS + h]) + n_mat[j * RW_HEADS + h] for h in heads]
        yn, bonus = [], []
        for h in heads:
            u = j * RW_HEADS + h
            mu = jnp.mean(ys[h], axis=-1, keepdims=True)
            var = jnp.mean(jnp.square(ys[h] - mu), axis=-1, keepdims=True)
            yn.append((ys[h] - mu) * lax.rsqrt(var + GN_EPS))
            bonus.append(jnp.sum(f32["rk"][u], axis=-1, keepdims=True) * f32["v"][u])
        yn = jnp.concatenate(yn, axis=-1)
        bonus = jnp.concatenate(bonus, axis=-1)
        rows = slice(j * c, (j + 1) * c)
        y_ref[0, rows, :] = ((yn * lnw_ref[...] + lnb_ref[...] + bonus) * g_ref[0, rows, :]).astype(BF)
    for h in heads:
        s_ref[h] = state[h]

    @pl.when(pl.program_id(1) == pl.num_programs(1) - 1)
    def _():
        sT_ref[0] = s_ref[...]


def _rwkv_chunk(r, lw, k, v, a, b, g, s0, p, c, n_sub):
    bsz, t, _ = r.shape
    tok = pl.BlockSpec((1, c * n_sub, RW_WIDTH), lambda i, j: (i, j, 0))
    par = pl.BlockSpec((1, RW_WIDTH), lambda i, j: (0, 0))
    st = pl.BlockSpec((1, RW_HEADS, RW_HEAD_DIM, RW_HEAD_DIM), lambda i, j: (i, 0, 0, 0))
    return pl.pallas_call(
        functools.partial(_rwkv_chunk_kernel, c=c),
        grid=(bsz, t // (c * n_sub)),
        in_specs=[tok] * 7 + [st, par, par, par],
        out_specs=[tok, st],
        out_shape=[jax.ShapeDtypeStruct((bsz, t, RW_WIDTH), BF),
                   jax.ShapeDtypeStruct((bsz, RW_HEADS, RW_HEAD_DIM, RW_HEAD_DIM), F32)],
        scratch_shapes=[pltpu.VMEM((RW_HEADS, RW_HEAD_DIM, RW_HEAD_DIM), F32)],
        compiler_params=_cp(("parallel", "arbitrary")),
        name="rwkv_chunk",
    )(r, lw, k, v, a, b, g, s0, p["ln_x_w"][None, :], p["ln_x_b"][None, :],
      p["r_k"].reshape(1, RW_WIDTH))


def _lambda_full(lq1, lk1, lq2, lk2):
    s1 = jnp.sum(lq1 * lk1, axis=-1, keepdims=True)
    s2 = jnp.sum(lq2 * lk2, axis=-1, keepdims=True)
    return jnp.exp(s1) - jnp.exp(s2) + LAMBDA_INIT


def _subln(o, g):
    y = o * lax.rsqrt(jnp.mean(o * o, axis=-1, keepdims=True) + NORM_EPS)
    return y * g * (1.0 - LAMBDA_INIT)


def _prompt_attn_kernel(qi_ref, ki_ref, q_ref, k_ref, v_ref, lq1, lk1, lq2, lk2, sg_ref,
                        o_ref, qm_ref, m_ref, l_ref, acc_ref):
    p = pl.program_id(2)
    qi = qi_ref[p]
    ki = ki_ref[p]
    tq = q_ref.shape[1]
    tk = k_ref.shape[1]

    @pl.when(ki == 0)
    def _():
        q = q_ref[0]
        lane = lax.broadcasted_iota(jnp.int32, q.shape, 1)
        zero = jnp.zeros_like(q)
        qm_ref[0:tq] = jnp.where(lane < DA_HEAD_DIM, q, zero)
        qm_ref[tq:2 * tq] = jnp.where(lane >= DA_HEAD_DIM, q, zero)
        m_ref[...] = jnp.full(m_ref.shape, -jnp.inf, F32)
        l_ref[...] = jnp.zeros(l_ref.shape, F32)
        acc_ref[...] = jnp.zeros(acc_ref.shape, F32)

    def step(masked):
        for c in range(2):
            rows = pl.ds(c * tq, tq)
            s = _dg(qm_ref[rows, :], k_ref[0], _NT)
            if masked:
                rowi = lax.broadcasted_iota(jnp.int32, (tq, tk), 0)
                coli = lax.broadcasted_iota(jnp.int32, (tq, tk), 1)
                s = jnp.where(coli <= rowi, s, -jnp.inf)
            m_old = m_ref[rows, :]
            m_new = jnp.maximum(m_old, jnp.max(s, axis=-1, keepdims=True))
            alpha = jnp.exp2(m_old - m_new)
            pr = jnp.exp2(s - jnp.concatenate([m_new] * (tk // 128), axis=-1))
            l_ref[rows, :] = alpha * l_ref[rows, :] + jnp.sum(pr, axis=-1, keepdims=True)
            acc_ref[rows, :] = alpha * acc_ref[rows, :] + _dot(pr.astype(BF), v_ref[0])
            m_ref[rows, :] = m_new

    @pl.when(ki < qi)
    def _():
        step(False)

    @pl.when(ki == qi)
    def _():
        step(True)
        lam = _lambda_full(lq1[...], lk1[...], lq2[...], lk2[...])
        o = acc_ref[...] / l_ref[...]
        o_ref[0] = _subln(o[0:tq] - lam * o[tq:2 * tq], sg_ref[...]).astype(BF)


def _prompt_attention(q, kb, vb, p, tq):
    bsz, t, _ = q.shape
    nq = t // tq
    qi = np.concatenate([np.full(i + 1, i) for i in range(nq)]).astype(np.int32)
    ki = np.concatenate([np.arange(i + 1) for i in range(nq)]).astype(np.int32)
    lam_spec = pl.BlockSpec((1, DA_HEAD_DIM), lambda b, h, s, qt, kt: (0, 0))
    grid_spec = pltpu.PrefetchScalarGridSpec(
        num_scalar_prefetch=2,
        grid=(bsz, DA_HEADS, len(qi)),
        in_specs=[pl.BlockSpec((1, tq, DA_V_DIM), lambda b, h, s, qt, kt: (b, qt[s], h)),
                  pl.BlockSpec((1, tq, DA_V_DIM), lambda b, h, s, qt, kt: (b, kt[s], h)),
                  pl.BlockSpec((1, tq, DA_V_DIM), lambda b, h, s, qt, kt: (b, kt[s], h)),
                  lam_spec, lam_spec, lam_spec, lam_spec,
                  pl.BlockSpec((1, DA_V_DIM), lambda b, h, s, qt, kt: (0, 0))],
        out_specs=pl.BlockSpec((1, tq, DA_V_DIM), lambda b, h, s, qt, kt: (b, qt[s], h)),
        scratch_shapes=[pltpu.VMEM((2 * tq, DA_V_DIM), BF),
                        pltpu.VMEM((2 * tq, 128), F32),
                        pltpu.VMEM((2 * tq, 128), F32),
                        pltpu.VMEM((2 * tq, DA_V_DIM), F32)])
    return pl.pallas_call(
        _prompt_attn_kernel,
        grid_spec=grid_spec,
        out_shape=jax.ShapeDtypeStruct((bsz, t, DA_WIDTH), BF),
        compiler_params=_cp(("parallel", "parallel", "arbitrary")),
        name="prompt_attn",
    )(jnp.asarray(qi), jnp.asarray(ki), q, kb, vb,
      p["lambda_q1"][None, :], p["lambda_k1"][None, :], p["lambda_q2"][None, :],
      p["lambda_k2"][None, :], p["subln_g"][None, :])


def _make_sample_attn_kernel(pages):
    n_pairs = DA_HEADS * 2
    groups = 1

    def kernel(pt_ref, q_ref, kn_ref, vn_ref, lq1, lk1, lq2, lk2, sg_ref, *rest):
        k_refs = rest[:pages]
        v_refs = rest[pages:2 * pages]
        o_ref = rest[2 * pages]
        qall_ref, m_ref, l_ref, acc_ref = rest[2 * pages + 1:]
        j = pl.program_id(1)
        s_len = q_ref.shape[1]
        rows = n_pairs * s_len
        page_rows = PAGE_SIZE * DA_HEADS

        @pl.when(j == 0)
        def _():
            q = q_ref[0].astype(F32)
            lane = lax.broadcasted_iota(jnp.int32, (s_len, DA_V_DIM), 1)
            pieces = []
            for h in range(DA_HEADS):
                qh = q[:, h * DA_V_DIM:(h + 1) * DA_V_DIM]
                pieces.append(jnp.where(lane < DA_HEAD_DIM, qh, 0.0))
                pieces.append(jnp.where(lane >= DA_HEAD_DIM, qh, 0.0))
            qall_ref[...] = jnp.concatenate(pieces, axis=0).astype(BF)
            m_ref[...] = jnp.full(m_ref.shape, -jnp.inf, F32)
            l_ref[...] = jnp.zeros(l_ref.shape, F32)
            acc_ref[...] = jnp.zeros(acc_ref.shape, F32)

        def partial_softmax(s, vals):
            m = jnp.max(s, axis=-1, keepdims=True)
            pr = jnp.exp2(s - m)
            return m, jnp.sum(pr, axis=-1, keepdims=True), _dot(pr.astype(BF), vals)

        def update(parts):
            m_old = m_ref[...]
            m_new = m_old
            for m, _, _ in parts:
                m_new = jnp.maximum(m_new, m)
            alpha = jnp.exp2(m_old - m_new)
            l_new = alpha * l_ref[...]
            acc = alpha * acc_ref[...]
            for m, l_part, acc_part in parts:
                w = jnp.exp2(m - m_new)
                l_new = l_new + w * l_part
                acc = acc + w * acc_part
            l_ref[...] = l_new
            acc_ref[...] = acc
            m_ref[...] = m_new

        qall = qall_ref[...]
        flat = lambda ref: ref[...].reshape(page_rows, DA_V_DIM).astype(BF)
        per_group = pages // groups
        parts = []
        for g in range(groups):
            sel = slice(g * per_group, (g + 1) * per_group)
            s_g = jnp.concatenate([_dg(qall, flat(kr), _NT) for kr in k_refs[sel]], axis=-1)
            v_g = jnp.concatenate([flat(vr) for vr in v_refs[sel]], axis=0)
            row_head = lax.broadcasted_iota(jnp.int32, s_g.shape, 0) // (2 * s_len)
            key_head = lax.broadcasted_iota(jnp.int32, s_g.shape, 1) % DA_HEADS
            parts.append(partial_softmax(jnp.where(row_head == key_head, s_g, -jnp.inf), v_g))
        update(parts)

        @pl.when(j == pl.num_programs(1) - 1)
        def _():
            kn = kn_ref[0].astype(F32)
            vn = vn_ref[0].astype(F32)
            by_head = lambda x: jnp.concatenate(
                [x[:, h * DA_V_DIM:(h + 1) * DA_V_DIM] for h in range(DA_HEADS)], axis=0).astype(BF)
            s = _dg(qall, by_head(kn), _NT)
            r = lax.broadcasted_iota(jnp.int32, s.shape, 0)
            c = lax.broadcasted_iota(jnp.int32, s.shape, 1)
            keep = (c // s_len == r // (2 * s_len)) & (c % s_len <= r % s_len)
            update([partial_softmax(jnp.where(keep, s, -jnp.inf), by_head(vn))])
            lam = _lambda_full(lq1[...], lk1[...], lq2[...], lk2[...])
            o = acc_ref[...] / l_ref[...]
            heads = []
            for h in range(DA_HEADS):
                o1 = o[(2 * h) * s_len:(2 * h + 1) * s_len]
                o2 = o[(2 * h + 1) * s_len:(2 * h + 2) * s_len]
                heads.append(_subln(o1 - lam * o2, sg_ref[...]))
            o_ref[0] = jnp.concatenate(heads, axis=-1).astype(BF)

    return kernel


def _sample_attention(q, kb, vb, cache_k, cache_v, page_table, p, pages):
    bsz, s_len, _ = q.shape
    n_pages = page_table.shape[1]
    rows = DA_HEADS * 2 * s_len
    tok = pl.BlockSpec((1, s_len, DA_WIDTH), lambda b, j, pt: (b, 0, 0))
    lam_spec = pl.BlockSpec((1, DA_HEAD_DIM), lambda b, j, pt: (0, 0))

    def page_spec(i):
        return pl.BlockSpec((None, None, PAGE_SIZE, DA_HEADS, DA_V_DIM),
                            lambda b, j, pt: (0, pt[b * n_pages + j * pages + i], 0, 0, 0))

    grid_spec = pltpu.PrefetchScalarGridSpec(
        num_scalar_prefetch=1,
        grid=(bsz, n_pages // pages),
        in_specs=[tok, tok, tok, lam_spec, lam_spec, lam_spec, lam_spec,
                  pl.BlockSpec((1, DA_V_DIM), lambda b, j, pt: (0, 0))]
                 + [page_spec(i) for i in range(pages)] * 2,
        out_specs=tok,
        scratch_shapes=[pltpu.VMEM((rows, DA_V_DIM), BF),
                        pltpu.VMEM((rows, 1), F32),
                        pltpu.VMEM((rows, 1), F32),
                        pltpu.VMEM((rows, DA_V_DIM), F32)])
    return pl.pallas_call(
        _make_sample_attn_kernel(pages),
        grid_spec=grid_spec,
        out_shape=jax.ShapeDtypeStruct((bsz, s_len, DA_WIDTH), BF),
        compiler_params=_cp(("parallel", "arbitrary")),
        name="sample_attn",
    )(page_table.reshape(-1), q, kb, vb,
      p["lambda_q1"][None, :], p["lambda_k1"][None, :], p["lambda_q2"][None, :],
      p["lambda_k2"][None, :], p["subln_g"][None, :], *([cache_k] * pages), *([cache_v] * pages))


def _merge_kernel(ya_ref, ob_ref, zg_ref, x_ref, wa_ref, wb_ref, wo_ref, g2_ref,
                  wr_hi_ref, wr_lo_ref, br_ref, h_ref, hn_ref, te_ref, tg_ref):
    zg = zg_ref[...]
    m = (_sigmoid(zg[:, :D_MODEL]) * _dot(ya_ref[...], wa_ref[...])
         + _sigmoid(zg[:, D_MODEL:]) * _dot(ob_ref[...], wb_ref[...]))
    h = x_ref[...] + _dot(m.astype(BF), wo_ref[...])
    h_ref[...] = h
    hn = h * lax.rsqrt(jnp.mean(h * h, axis=-1, keepdims=True) + NORM_EPS) * g2_ref[...]
    tm = hn.shape[0]
    for j in range(ROW_TILES):
        hn_ref[pl.ds(j, tm, stride=ROW_TILES), :] = hn[:, j * 128:(j + 1) * 128]

    hh, hl = _split(hn)
    logits = (_dg(wr_hi_ref[...], hh, _NT) + _dg(wr_hi_ref[...], hl, _NT)
              + _dg(wr_lo_ref[...], hh, _NT)) + br_ref[...]
    eidx = lax.broadcasted_iota(jnp.int32, logits.shape, 0)
    vals, idxs = [], []
    for _ in range(TOP_K):
        mx = jnp.max(logits, axis=0, keepdims=True)
        am = jnp.min(jnp.where(logits == mx, eidx, N_EXPERTS), axis=0, keepdims=True)
        vals.append(mx)
        idxs.append(am)
        logits = jnp.where(eidx == am, -jnp.inf, logits)
    ex = [jnp.exp(vv - vals[0]) for vv in vals]
    den = ex[0] + ex[1] + ex[2] + ex[3]
    te_ref[...] = jnp.concatenate(idxs, axis=0)
    tg_ref[...] = jnp.concatenate([e / den for e in ex], axis=0)


def _merge(ya, ob, zg, x2, p, tm):
    n = x2.shape[0]
    wr_t = p["w_router"].T
    wr_hi = wr_t.astype(BF)
    wr_lo = (wr_t - wr_hi.astype(F32)).astype(BF)
    row = lambda w: pl.BlockSpec((tm, w), lambda i: (i, 0))
    mat = lambda r, c: pl.BlockSpec((r, c), lambda i: (0, 0))
    colblk = pl.BlockSpec((TOP_K, tm), lambda i: (0, i))
    return pl.pallas_call(
        _merge_kernel,
        grid=(n // tm,),
        in_specs=[row(RW_WIDTH), row(DA_WIDTH), row(2 * D_MODEL), row(D_MODEL),
                  mat(RW_WIDTH, D_MODEL), mat(DA_WIDTH, D_MODEL), mat(D_MODEL, D_MODEL),
                  mat(1, D_MODEL), mat(N_EXPERTS, D_MODEL), mat(N_EXPERTS, D_MODEL),
                  mat(N_EXPERTS, 1)],
        out_specs=[row(D_MODEL), pl.BlockSpec((tm * ROW_TILES, 128), lambda i: (i, 0)),
                   colblk, colblk],
        out_shape=[jax.ShapeDtypeStruct((n, D_MODEL), F32),
                   jax.ShapeDtypeStruct((n * ROW_TILES, 128), F32),
                   jax.ShapeDtypeStruct((TOP_K, n), jnp.int32),
                   jax.ShapeDtypeStruct((TOP_K, n), F32)],
        compiler_params=_cp(("parallel",)),
        name="merge_router",
    )(ya, ob, zg, x2, p["w_a"].astype(BF), p["w_b"].astype(BF), p["w_o"].astype(BF),
      p["norm2_g"][None, :], wr_hi, wr_lo, p["b_router"][:, None])


def _experts_kernel(ib_ref, ie_ref, ilo_ref, ihi_ref, ifirst_ref, ilast_ref, ifused_ref,
                    idx_hbm, x_hbm, wu_ref, bu_ref, wd_ref, bd_ref, out_hbm,
                    idx_smem, idx_sem, xbuf, gsem, xb_ref, acc_ref, obuf, ssem):
    i = pl.program_id(0)
    bm = MOE_ROWS
    slab = bm * ROW_TILES
    n_blocks = idx_hbm.shape[0]
    blk = ib_ref[i]
    s2 = blk % 2

    def idx_copy(b):
        slot = b % IDX_SLOTS
        return pltpu.make_async_copy(idx_hbm.at[b], idx_smem.at[pl.ds(slot * 2 * bm, 2 * bm)],
                                     idx_sem.at[slot])

    def tile_rows(row):
        return pl.ds(pl.multiple_of(row * ROW_TILES, ROW_TILES), ROW_TILES)

    def gather_row(b, r):
        tok = idx_smem[(b % IDX_SLOTS) * 2 * bm + r]
        return pltpu.make_async_copy(x_hbm.at[tile_rows(tok)], xbuf.at[tile_rows((b % 2) * bm + r)],
                                     gsem.at[b % 2])

    def scatter_row(b, r):
        dst = idx_smem[(b % IDX_SLOTS) * 2 * bm + bm + r]
        return pltpu.make_async_copy(obuf.at[tile_rows((b % 2) * bm + r)], out_hbm.at[tile_rows(dst)],
                                     ssem.at[b % 2])

    def issue(copy, b, unrolled):
        if unrolled:
            for r in range(bm):
                copy(b, r).start()
        else:
            def body(r, carry):
                copy(b, r).start()
                return carry
            lax.fori_loop(0, bm, body, 0, unroll=8)

    def block_rows(slot2):
        return pl.ds(pl.multiple_of(slot2 * slab, slab), slab)

    def wait_gather(slot2):
        pltpu.make_async_copy(x_hbm.at[pl.ds(0, slab)], xbuf.at[block_rows(slot2)], gsem.at[slot2]).wait()

    def wait_scatter(slot2):
        pltpu.make_async_copy(obuf.at[block_rows(slot2)], out_hbm.at[pl.ds(0, slab)], ssem.at[slot2]).wait()

    def load_tokens():
        base = s2 * slab
        x = jnp.concatenate([xbuf[pl.ds(base + j, bm, stride=ROW_TILES), :] for j in range(ROW_TILES)],
                            axis=-1)
        return x.astype(BF)

    def mlp(x):
        h = _dot(x, wu_ref[...]) + bu_ref[...]
        hg = jnp.minimum(h[:, :D_FF], SWIGLU_LIMIT)
        hl = jnp.clip(h[:, D_FF:], -SWIGLU_LIMIT, SWIGLU_LIMIT)
        act = hg * _sigmoid(SWIGLU_ALPHA * hg) * (hl + 1.0)
        return _dot(act.astype(BF), wd_ref[...]) + bd_ref[...]

    def store_rows(y):
        base = s2 * slab
        for j in range(ROW_TILES):
            obuf[pl.ds(base + j, bm, stride=ROW_TILES), :] = y[:, j * 128:(j + 1) * 128]

    first = ifirst_ref[i] == 1
    fused = ifused_ref[i] == 1

    @pl.when(i == 0)
    def _():
        idx_copy(0).start()
        idx_copy(0).wait()
        issue(gather_row, 0, False)
        if n_blocks > 1:
            idx_copy(1).start()

    @pl.when(first & (blk + 2 < n_blocks))
    def _():
        idx_copy(blk + 2).start()

    @pl.when(fused)
    def _():
        wait_gather(s2)
        x = load_tokens()
        issue(scatter_row, blk - 1, True)
        idx_copy(blk + 1).wait()
        issue(gather_row, blk + 1, True)
        y = mlp(x)
        wait_scatter(s2)
        store_rows(y)

    @pl.when(jnp.logical_not(fused))
    def _():
        @pl.when(first)
        def _():
            wait_gather(s2)

            @pl.when(blk >= 1)
            def _():
                issue(scatter_row, blk - 1, False)

            @pl.when(blk + 1 < n_blocks)
            def _():
                idx_copy(blk + 1).wait()
                issue(gather_row, blk + 1, False)

            xb_ref[...] = load_tokens()
            acc_ref[...] = jnp.zeros(acc_ref.shape, F32)

        lo = ilo_ref[i]
        hi = ihi_ref[i]

        @pl.when(hi > lo)
        def _():
            y = mlp(xb_ref[...])
            rowi = lax.broadcasted_iota(jnp.int32, (bm, 1), 0)
            acc_ref[...] += jnp.where((rowi >= lo) & (rowi < hi), y, 0.0)

        @pl.when(ilast_ref[i] == 1)
        def _():
            @pl.when(blk >= 2)
            def _():
                wait_scatter(s2)

            store_rows(acc_ref[...])

    @pl.when(i == pl.num_programs(0) - 1)
    def _():
        issue(scatter_row, n_blocks - 1, False)
        wait_scatter((n_blocks - 1) % 2)
        if n_blocks > 1:
            wait_scatter((n_blocks - 2) % 2)


def _experts(hn_rows, items, idx, w_up, b_up, w_down, b_down):
    bm = MOE_ROWS
    n_blocks = idx.shape[0]
    n_items = items[0].shape[0]
    wspec = lambda r, c: pl.BlockSpec((None, r, c), lambda i, ib, ie, *_: (ie[i], 0, 0))
    grid_spec = pltpu.PrefetchScalarGridSpec(
        num_scalar_prefetch=7,
        grid=(n_items,),
        in_specs=[pl.BlockSpec(memory_space=pl.ANY), pl.BlockSpec(memory_space=pl.ANY),
                  wspec(D_MODEL, 2 * D_FF), wspec(1, 2 * D_FF), wspec(D_FF, D_MODEL), wspec(1, D_MODEL)],
        out_specs=pl.BlockSpec(memory_space=pl.ANY),
        scratch_shapes=[pltpu.SMEM((IDX_SLOTS * 2 * bm,), jnp.int32),
                        pltpu.SemaphoreType.DMA((IDX_SLOTS,)),
                        pltpu.VMEM((2 * bm * ROW_TILES, 128), F32),
                        pltpu.SemaphoreType.DMA((2,)),
                        pltpu.VMEM((bm, D_MODEL), BF),
                        pltpu.VMEM((bm, D_MODEL), F32),
                        pltpu.VMEM((2 * bm * ROW_TILES, 128), F32),
                        pltpu.SemaphoreType.DMA((2,))])
    return pl.pallas_call(
        _experts_kernel,
        grid_spec=grid_spec,
        out_shape=jax.ShapeDtypeStruct((n_blocks * bm * ROW_TILES, 128), F32),
        compiler_params=_cp(("arbitrary",)),
        name="experts",
    )(*items, idx, hn_rows, w_up, b_up[:, None, :], w_down, b_down[:, None, :])


def _combine_kernel(h_ref, g_ref, *rest):
    e_refs = rest[:TOP_K]
    o_ref = rest[TOP_K]
    tm = h_ref.shape[0]
    g = g_ref[...]
    for j in range(ROW_TILES):
        acc = h_ref[:, j * 128:(j + 1) * 128]
        for kk in range(TOP_K):
            acc = acc + g[:, kk:kk + 1] * e_refs[kk][pl.ds(j, tm, stride=ROW_TILES), :]
        o_ref[:, j * 128:(j + 1) * 128] = acc


def _combine(h, gates_t, expert_rows, tm):
    n = h.shape[0]
    nt = n // tm
    slot = lambda kk: pl.BlockSpec((tm * ROW_TILES, 128), lambda i: (kk * nt + i, 0))
    return pl.pallas_call(
        _combine_kernel,
        grid=(nt,),
        in_specs=[pl.BlockSpec((tm, D_MODEL), lambda i: (i, 0)),
                  pl.BlockSpec((tm, TOP_K), lambda i: (i, 0))]
                 + [slot(kk) for kk in range(TOP_K)],
        out_specs=pl.BlockSpec((tm, D_MODEL), lambda i: (i, 0)),
        out_shape=jax.ShapeDtypeStruct((n, D_MODEL), F32),
        compiler_params=_cp(("parallel",)),
        name="combine",
    )(h, gates_t, *([expert_rows] * TOP_K))


def _route(top_e):
    n = top_e.shape[1]
    nk = n * TOP_K
    bm = MOE_ROWS
    n_blocks = nk // bm
    n_items = n_blocks + N_EXPERTS
    flat_e = top_e.reshape(-1)
    _, sorted_slot = lax.sort((flat_e, jnp.arange(nk, dtype=jnp.int32)), num_keys=1)
    experts = jnp.arange(N_EXPERTS, dtype=jnp.int32)
    counts = jnp.sum((flat_e[None, :] == experts[:, None]).astype(jnp.int32), axis=1)
    end = jnp.cumsum(counts)
    start = end - counts
    first_blk = start // bm
    n_e = jnp.where(counts > 0, (end - 1) // bm - first_blk + 1, 0)
    cum = jnp.cumsum(n_e)
    off = cum - n_e
    total = cum[-1]
    it = jnp.arange(n_items, dtype=jnp.int32)
    valid = it < total
    e_of = lambda t: jnp.minimum(jnp.sum((cum[None, :] <= t[:, None]).astype(jnp.int32), axis=1),
                                 N_EXPERTS - 1)
    e_i = jnp.where(valid, e_of(it), e_of(total[None] - 1)[0])
    blk = jnp.where(valid, first_blk[e_i] + it - off[e_i], n_blocks - 1)
    lo = jnp.where(valid, jnp.clip(start[e_i] - blk * bm, 0, bm), 0)
    hi = jnp.where(valid, jnp.clip(end[e_i] - blk * bm, 0, bm), 0)
    prev_blk = jnp.concatenate([jnp.full((1,), -1, jnp.int32), blk[:-1]])
    next_blk = jnp.concatenate([blk[1:], jnp.full((1,), -1, jnp.int32)])
    first = valid & (blk != prev_blk)
    last = valid & ((blk != next_blk) | (it == total - 1))
    fused = first & last & (blk >= 2) & (blk + 1 < n_blocks)
    items = tuple(x.astype(jnp.int32) for x in (blk, e_i, lo, hi, first, last, fused))
    idx = jnp.concatenate([(sorted_slot % n).reshape(n_blocks, bm), sorted_slot.reshape(n_blocks, bm)],
                          axis=1)
    return items, idx


def _moe(h, hn_rows, top_e, top_g, wts, tm):
    items, idx = _route(top_e)
    expert_rows = _experts(hn_rows, items, idx, *wts)
    return _combine(h, top_g.T, expert_rows, tm)


def _pick(n, pref):
    t = min(n, pref)
    assert n % t == 0
    return t


def _layer(x, pos, s0, shift0, p, w_in_bf, moe_wts, attend, chunk):
    b, t, _ = x.shape
    n = b * t
    tm = _pick(n, 256)
    tb = _pick(n, 512)
    tt = _pick(t, 512)
    x2 = x.reshape(n, D_MODEL)
    zr, zq, zk, zv, zvb, zg = _inproj(x2, p["norm1_g"][None, :], w_in_bf, tm)
    zr3 = zr.reshape(b, t, RW_IN)
    q, k, kb = _qk_prep(zq.reshape(b, t, DA_WIDTH), zk.reshape(b, t, DA_WIDTH),
                        p["q_norm_g"], p["k_norm_g"], pos, tt)
    r, lw, km, v, a, bb, g = _rwkv_prep(zr3, shift0, p, tt)
    ya, s_t = _rwkv_chunk(r, lw, km, v, a, bb, g, s0, p, chunk, _pick(t // chunk, 4))
    ob = attend(q, kb, zvb.reshape(b, t, DA_WIDTH))
    h, hn, top_e, top_g = _merge(ya.reshape(n, RW_WIDTH), ob.reshape(n, DA_WIDTH), zg, x2, p, tb)
    y = _moe(h, hn, top_e, top_g, moe_wts, tb)
    return (y.reshape(b, t, D_MODEL), k.reshape(b, t, DA_HEADS, 2 * DA_HEAD_DIM),
            zv.reshape(b, t, DA_HEADS, DA_V_DIM), s_t, zr3[:, -1])


def kernel(x_prompt, x_sample, cache_k, cache_v, page_table, state_wkv, state_shift, norm1_g, w_in, mu_shift, w0, w_w2, a0, a_a2, g_g2, k_k, k_a, r_k, ln_x_w, ln_x_b, q_norm_g, k_norm_g, lambda_q1, lambda_k1, lambda_q2, lambda_k2, subln_g, w_a, w_b, w_o, norm2_g, w_router, b_router, w_up, b_up, w_down, b_down):
    names = ["norm1_g", "mu_shift", "w0", "w_w2", "a0", "a_a2", "g_g2", "k_k", "k_a", "r_k",
             "ln_x_w", "ln_x_b", "q_norm_g", "k_norm_g", "lambda_q1", "lambda_k1", "lambda_q2",
             "lambda_k2", "subln_g", "w_a", "w_b", "w_o", "norm2_g", "w_router", "b_router"]
    vals = [norm1_g, mu_shift, w0, w_w2, a0, a_a2, g_g2, k_k, k_a, r_k, ln_x_w, ln_x_b,
            q_norm_g, k_norm_g, lambda_q1, lambda_k1, lambda_q2, lambda_k2, subln_g,
            w_a, w_b, w_o, norm2_g, w_router, b_router]
    p = {nm: vv[0] for nm, vv in zip(names, vals)}
    w_in_bf = w_in[0].astype(BF)
    moe_wts = (w_up[0].astype(BF), b_up[0], w_down[0].astype(BF), b_down[0])

    bp, tp, _ = x_prompt.shape
    bs, ts, _ = x_sample.shape
    past = page_table.shape[1] * PAGE_SIZE

    attend_p = lambda q, kb, vb: _prompt_attention(q, kb, vb, p, _pick(tp, 1024))
    yp, kp, vp, wp, sp = _layer(
        x_prompt, jnp.arange(tp), jnp.zeros((bp, RW_HEADS, RW_HEAD_DIM, RW_HEAD_DIM), F32),
        jnp.zeros((bp, RW_IN), F32), p, w_in_bf, moe_wts, attend_p, _pick(tp, 64))

    attend_s = lambda q, kb, vb: _sample_attention(
        q, kb, vb, cache_k, cache_v, page_table, p, _pick(page_table.shape[1], 8))
    ys, ks, vs, ws, ss = _layer(
        x_sample, past + jnp.arange(ts), state_wkv[0], state_shift[0], p, w_in_bf, moe_wts,
        attend_s, ts)

    return (yp, ys, kp[None], vp[None], wp[None], sp[None],
            ks[None], vs[None], ws[None], ss[None])
```
